```python
import math
import jax
import jax.numpy as jnp
from jax import lax
import numpy as np

D_MODEL = 2048
BATCH = 4
SEQ = 2048
DEPTH = 4
DEC_BATCH = 128
DEC_SEQ = 1
PAST_LEN = 16384
PAGE_SIZE = 128

N_EVEN = (DEPTH + 1) // 2
N_ODD = DEPTH // 2
POOL_CH = D_MODEL // 2
POOL_WINDOWS = (2, 4, 8, 16)
POOL_GROUPS = len(POOL_WINDOWS)
POOL_GROUP_CH = POOL_CH // POOL_GROUPS
POOL_HIST = max(POOL_WINDOWS) - 1
SGU_CH = D_MODEL // 2
SGU_HEADS = 4
SGU_HEAD_CH = SGU_CH // SGU_HEADS
SGU_CHUNK = 128
EVEN_IN = POOL_CH + 2 * SGU_CH
EVEN_OUT = POOL_CH + SGU_CH
CONV_CH = D_MODEL // 2
CONV_WIDTH = 31
DN_HEADS = 8
DN_DK = 128
DN_DV = 128
DN_QK = DN_HEADS * DN_DK
DN_V = DN_HEADS * DN_DV
DN_CONV_CH = 2 * DN_QK + DN_V
DN_CONV = 4
DN_CHUNK = 64
ODD_IN = 2 * CONV_CH + DN_CONV_CH + DN_V + 2 * DN_HEADS
ODD_OUT = CONV_CH + DN_V
D_FF = -(-8 * D_MODEL // (3 * 256)) * 256
EPS = 1e-6

kernel_name = 'hybrid_pool_sgu_conformer_gdn_decode_step'


def _rmsnorm(x, g):
    xf = x.astype(jnp.float32)
    y = xf * lax.rsqrt(jnp.mean(xf * xf, -1, keepdims=True) + EPS)
    return (y * g.astype(jnp.float32)).astype(x.dtype)


def _layernorm(x, g, b):
    xf = x.astype(jnp.float32)
    mu = jnp.mean(xf, -1, keepdims=True)
    var = jnp.mean(jnp.square(xf - mu), -1, keepdims=True)
    y = (xf - mu) * lax.rsqrt(var + EPS) * g.astype(jnp.float32) + b.astype(jnp.float32)
    return y.astype(x.dtype)


def _l2norm(x):
    return x * lax.rsqrt(jnp.sum(x * x, -1, keepdims=True) + EPS)


def _causal_dwconv(xh, w):
    ch = w.shape[1]
    return lax.conv_general_dilated(xh, w[:, None, :].astype(xh.dtype), window_strides=(1,), padding='VALID',
                                    dimension_numbers=('NWC', 'WIO', 'NWC'), feature_group_count=ch)


def _pool_mixer(a, hist, pos0, w_grp, scale):
    bsz, t_len, ch = a.shape
    xh = jnp.concatenate([hist.astype(a.dtype), a], 1)
    ah = xh.astype(jnp.float32)
    cs = jnp.concatenate([jnp.zeros((bsz, 1, ch), jnp.float32), jnp.cumsum(ah, axis=1)], 1)
    pos = pos0 + jnp.arange(t_len)
    off = POOL_HIST + 1
    outs = []
    for gi, w in enumerate(POOL_WINDOWS):
        sl = slice(gi * POOL_GROUP_CH, (gi + 1) * POOL_GROUP_CH)
        s = cs[:, off:off + t_len, sl] - cs[:, off - w:off - w + t_len, sl]
        cnt = jnp.minimum(w, pos + 1).astype(jnp.float32)[None, :, None]
        outs.append(s / cnt)
    pooled = (jnp.concatenate(outs, -1) - a.astype(jnp.float32)).astype(a.dtype)
    pooled = pooled.reshape(bsz, t_len, POOL_GROUPS, POOL_GROUP_CH)
    mixed = jnp.einsum('btgc,gcd->btgd', pooled, w_grp).reshape(bsz, t_len, ch)
    return mixed * scale, xh[:, -POOL_HIST:]


def _sgu_mixer(u, v, ln_g, ln_b, ws, bs):
    bsz, t_len, _ = u.shape
    vn = _layernorm(v, ln_g, ln_b)
    n = -(-t_len // SGU_CHUNK)
    tp = n * SGU_CHUNK
    vp = jnp.pad(vn, ((0, 0), (0, tp - t_len), (0, 0))).reshape(bsz, n, SGU_CHUNK, SGU_HEADS, SGU_HEAD_CH)
    causal = jnp.tril(jnp.ones((SGU_CHUNK, SGU_CHUNK), bool))
    wm = jnp.where(causal[None], ws, jnp.zeros_like(ws))
    mixed = jnp.einsum('hij,bnjhc->bnihc', wm, vp) + bs.T[None, None, :, :, None]
    mixed = mixed.reshape(bsz, tp, SGU_CH)[:, :t_len]
    return u * mixed, vn


def _conv_module(zc, hist, dw, db, ln_g, ln_b):
    glu = zc[..., :CONV_CH] * jax.nn.sigmoid(zc[..., CONV_CH:])
    xh = jnp.concatenate([hist.astype(glu.dtype), glu], 1)
    y = _causal_dwconv(xh, dw) + db
    y = jax.nn.silu(_layernorm(y, ln_g, ln_b))
    return y, xh[:, -(CONV_WIDTH - 1):]


def _to_chunks(t, n, c):
    bsz = t.shape[0]
    t = t.reshape((bsz, n, c) + t.shape[2:])
    return t.transpose((1, 0, 3, 2) + tuple(range(4, t.ndim)))


def _gated_delta(q, k, v, g, beta, s0):
    bsz, t_len, nh, _ = q.shape
    c = min(DN_CHUNK, t_len)
    n = -(-t_len // c)
    tp = n * c

    def padt(t):
        return jnp.pad(t, [(0, 0), (0, tp - t_len)] + [(0, 0)] * (t.ndim - 2))

    q, k, v, g, beta = [_to_chunks(padt(t), n, c) for t in (q, k, v, g, beta)]
    gc = jnp.cumsum(g, axis=-1)
    lower = jnp.tril(jnp.ones((c, c), bool))
    strict = jnp.tril(jnp.ones((c, c), bool), -1)
    decay = jnp.exp(jnp.where(lower, gc[..., :, None] - gc[..., None, :], -jnp.inf))
    kb = k * beta[..., None]
    vb = v * beta[..., None]
    lmat = jnp.where(strict, jnp.einsum('nbhid,nbhjd->nbhij', kb, k) * decay, 0.0)
    amat = lmat + jnp.eye(c, dtype=jnp.float32)
    rhs = jnp.concatenate([vb, kb * jnp.exp(gc)[..., None]], -1)
    sol = lax.linalg.triangular_solve(amat, rhs, left_side=True, lower=True, unit_diagonal=True)
    u_c, w_c = sol[..., :DN_DV], sol[..., DN_DV:]
    qk = jnp.einsum('nbhid,nbhjd->nbhij', q, k) * decay
    q_dec = q * jnp.exp(gc)[..., None]
    k_dec = k * jnp.exp(gc[..., -1:] - gc)[..., None]
    g_last = jnp.exp(gc[..., -1])

    def step(s, xs):
        u_i, w_i, qk_i, qd_i, kd_i, gl_i = xs
        v_new = u_i - jnp.einsum('bhcd,bhde->bhce', w_i, s)
        o = jnp.einsum('bhcd,bhde->bhce', qd_i, s) + jnp.einsum('bhij,bhje->bhie', qk_i, v_new)
        s = s * gl_i[..., None, None] + jnp.einsum('bhcd,bhce->bhde', kd_i, v_new)
        return s, o

    s_fin, o = lax.scan(step, s0, (u_c, w_c, qk, q_dec, k_dec, g_last))
    o = o.transpose(1, 0, 3, 2, 4).reshape(bsz, tp, nh, DN_DV)[:, :t_len]
    return o, s_fin


def _gated_deltanet(qkv, zg, a, b, hist, s0, conv_w, a_log, dt_bias, norm_g):
    bsz, t_len, _ = qkv.shape
    xh = jnp.concatenate([hist.astype(qkv.dtype), qkv], 1)
    qkv_c = jax.nn.silu(_causal_dwconv(xh, conv_w)).astype(jnp.float32)
    q = qkv_c[..., :DN_QK].reshape(bsz, t_len, DN_HEADS, DN_DK)
    k = qkv_c[..., DN_QK:2 * DN_QK].reshape(bsz, t_len, DN_HEADS, DN_DK)
    v = qkv_c[..., 2 * DN_QK:].reshape(bsz, t_len, DN_HEADS, DN_DV)
    q = _l2norm(q) * (DN_DK ** -0.5)
    k = _l2norm(k)
    g = -jnp.exp(a_log.astype(jnp.float32)) * jax.nn.softplus(a.astype(jnp.float32) + dt_bias.astype(jnp.float32))
    beta = jax.nn.sigmoid(b.astype(jnp.float32))
    o, s_fin = _gated_delta(q, k, v, g, beta, s0.astype(jnp.float32))
    o = _rmsnorm(o, norm_g) * jax.nn.silu(zg.astype(jnp.float32).reshape(bsz, t_len, DN_HEADS, DN_DV))
    return o.reshape(bsz, t_len, DN_V).astype(qkv.dtype), xh[:, -(DN_CONV - 1):], s_fin.astype(s0.dtype)


def _trunk(x, pool_st, convc_st, dnconv_st, dns_st, pos0, p):
    new_pool, new_v, new_cc, new_dc, new_s = [], [], [], [], []
    for l in range(DEPTH):
        h = _rmsnorm(x, p['norm_mix'][l])
        if l % 2 == 0:
            e = l // 2
            z = h @ p['ev_w_in'][e]
            a_out, pool_hist = _pool_mixer(z[..., :POOL_CH], pool_st[e], pos0, p['pool_w'][e], p['pool_scale'][e])
            uv = jax.nn.gelu(z[..., POOL_CH:], approximate=False)
            b_out, v_rows = _sgu_mixer(uv[..., :SGU_CH], uv[..., SGU_CH:], p['sgu_ln_g'][e], p['sgu_ln_b'][e],
                                       p['sgu_ws'][e], p['sgu_b'][e])
            mix = jnp.concatenate([a_out, b_out], -1) @ p['ev_w_out'][e]
            new_pool.append(pool_hist)
            new_v.append(v_rows)
        else:
            o = l // 2
            z = h @ p['od_w_in'][o]
            c0 = 2 * CONV_CH
            c1 = c0 + DN_CONV_CH
            c2 = c1 + DN_V
            c3 = c2 + DN_HEADS
            c_out, cc = _conv_module(z[..., :c0], convc_st[o], p['cv_dw'][o], p['cv_db'][o], p['cv_ln_g'][o], p['cv_ln_b'][o])
            d_out, dc, s = _gated_deltanet(z[..., c0:c1], z[..., c1:c2], z[..., c2:c3], z[..., c3:], dnconv_st[o], dns_st[o],
                                           p['dn_conv_w'][o], p['dn_a_log'][o], p['dn_dt_bias'][o], p['dn_norm_g'][o])
            mix = jnp.concatenate([c_out, d_out], -1) @ p['od_w_out'][o]
            new_cc.append(cc)
            new_dc.append(dc)
            new_s.append(s)
        x = x + mix
        h = _rmsnorm(x, p['norm_ffn'][l])
        gu = h @ p['ffn_w_up'][l]
        x = x + (jax.nn.silu(gu[..., :D_FF]) * gu[..., D_FF:]) @ p['ffn_w_down'][l]
    y = _rmsnorm(x, p['norm_final'])
    return y, jnp.stack(new_pool), jnp.stack(new_v), jnp.stack(new_cc), jnp.stack(new_dc), jnp.stack(new_s)


def setup_inputs(seed: int = 0) -> dict:
    key = jax.random.key(seed)
    ks = iter(jax.random.split(key, 64))

    def nrm(shape, scale):
        return jax.random.normal(next(ks), shape, jnp.float32) * scale

    def gain(shape):
        return 1.0 + nrm(shape, 0.05)

    dt = jnp.exp(jax.random.uniform(next(ks), (N_ODD, DN_HEADS), jnp.float32, math.log(1e-3), math.log(1e-1)))
    return {
        'x_prompt': nrm((BATCH, SEQ, D_MODEL), 1.0),
        'x_sample': nrm((DEC_BATCH, DEC_SEQ, D_MODEL), 1.0),
        'state_pool': nrm((N_EVEN, DEC_BATCH, POOL_HIST, POOL_CH), 1.0),
        'state_conv_c': nrm((N_ODD, DEC_BATCH, CONV_WIDTH - 1, CONV_CH), 0.5),
        'state_dn_conv': nrm((N_ODD, DEC_BATCH, DN_CONV - 1, DN_CONV_CH), 1.0),
        'state_dn_S': nrm((N_ODD, DEC_BATCH, DN_HEADS, DN_DK, DN_DV), 0.5),
        'norm_mix': gain((DEPTH, D_MODEL)),
        'norm_ffn': gain((DEPTH, D_MODEL)),
        'norm_final': gain((D_MODEL,)),
        'ev_w_in': nrm((N_EVEN, D_MODEL, EVEN_IN), D_MODEL ** -0.5),
        'pool_w': nrm((N_EVEN, POOL_GROUPS, POOL_GROUP_CH, POOL_GROUP_CH), POOL_GROUP_CH ** -0.5),
        'pool_scale': gain((N_EVEN, POOL_CH)),
        'sgu_ln_g': gain((N_EVEN, SGU_CH)),
        'sgu_ln_b': nrm((N_EVEN, SGU_CH), 0.02),
        'sgu_ws': nrm((N_EVEN, SGU_HEADS, SGU_CHUNK, SGU_CHUNK), SGU_CHUNK ** -0.5),
        'sgu_b': 1.0 + nrm((N_EVEN, SGU_HEADS, SGU_CHUNK), 0.02),
        'ev_w_out': nrm((N_EVEN, EVEN_OUT, D_MODEL), EVEN_OUT ** -0.5),
        'od_w_in': nrm((N_ODD, D_MODEL, ODD_IN), D_MODEL ** -0.5),
        'cv_dw': nrm((N_ODD, CONV_WIDTH, CONV_CH), CONV_WIDTH ** -0.5),
        'cv_db': nrm((N_ODD, CONV_CH), 0.02),
        'cv_ln_g': gain((N_ODD, CONV_CH)),
        'cv_ln_b': nrm((N_ODD, CONV_CH), 0.02),
        'dn_conv_w': nrm((N_ODD, DN_CONV, DN_CONV_CH), DN_CONV ** -0.5),
        'dn_a_log': jnp.log(jax.random.uniform(next(ks), (N_ODD, DN_HEADS), jnp.float32, 1.0, 16.0)),
        'dn_dt_bias': dt + jnp.log(-jnp.expm1(-dt)),
        'dn_norm_g': gain((N_ODD, DN_DV)),
        'od_w_out': nrm((N_ODD, ODD_OUT, D_MODEL), ODD_OUT ** -0.5),
        'ffn_w_up': nrm((DEPTH, D_MODEL, 2 * D_FF), D_MODEL ** -0.5),
        'ffn_w_down': nrm((DEPTH, D_FF, D_MODEL), D_FF ** -0.5),
    }


def reference(x_prompt, x_sample, state_pool, state_conv_c, state_dn_conv, state_dn_S,
              norm_mix, norm_ffn, norm_final, ev_w_in, pool_w, pool_scale, sgu_ln_g, sgu_ln_b, sgu_ws, sgu_b,
              ev_w_out, od_w_in, cv_dw, cv_db, cv_ln_g, cv_ln_b, dn_conv_w, dn_a_log, dn_dt_bias, dn_norm_g,
              od_w_out, ffn_w_up, ffn_w_down):
    p = dict(norm_mix=norm_mix, norm_ffn=norm_ffn, norm_final=norm_final, ev_w_in=ev_w_in, pool_w=pool_w,
             pool_scale=pool_scale, sgu_ln_g=sgu_ln_g, sgu_ln_b=sgu_ln_b, sgu_ws=sgu_ws, sgu_b=sgu_b,
             ev_w_out=ev_w_out, od_w_in=od_w_in, cv_dw=cv_dw, cv_db=cv_db, cv_ln_g=cv_ln_g, cv_ln_b=cv_ln_b,
             dn_conv_w=dn_conv_w, dn_a_log=dn_a_log, dn_dt_bias=dn_dt_bias, dn_norm_g=dn_norm_g,
             od_w_out=od_w_out, ffn_w_up=ffn_w_up, ffn_w_down=ffn_w_down)
    bp = x_prompt.shape[0]
    dt = x_prompt.dtype
    pool0 = jnp.zeros((N_EVEN, bp, POOL_HIST, POOL_CH), dt)
    cc0 = jnp.zeros((N_ODD, bp, CONV_WIDTH - 1, CONV_CH), dt)
    dc0 = jnp.zeros((N_ODD, bp, DN_CONV - 1, DN_CONV_CH), dt)
    s0 = jnp.zeros((N_ODD, bp, DN_HEADS, DN_DK, DN_DV), state_dn_S.dtype)
    y_prompt, pool_p, _, cc_p, dc_p, s_p = _trunk(x_prompt, pool0, cc0, dc0, s0, 0, p)
    y_sample, pool_s, v_s, cc_s, dc_s, s_s = _trunk(x_sample, state_pool, state_conv_c, state_dn_conv, state_dn_S,
                                                    PAST_LEN, p)
    return (y_prompt, y_sample, pool_p, pool_s, v_s, cc_p, cc_s, dc_p, dc_s, s_p, s_s)
```

```python
import functools

import jax
import jax.numpy as jnp
from jax import lax
from jax.experimental import pallas as pl
from jax.experimental.pallas import tpu as pltpu

F32 = jnp.float32
BF16 = jnp.bfloat16
HI = lax.Precision.HIGHEST

EPS = 1e-6
PAST_LEN = 16384
POOL_WINDOWS = (2, 4, 8, 16)
POOL_HIST = max(POOL_WINDOWS) - 1
SGU_HEADS = 4
SGU_CHUNK = 128
CONV_WIDTH = 31
DN_HEADS = 8
DN_DK = 128
DN_CONV = 4
DN_CHUNK = 64
DN_SOLVE_BLOCK = 16

LANES = 128
VMEM_LIMIT = 48 * 1024 * 1024


def _params(*semantics):
    return pltpu.CompilerParams(dimension_semantics=semantics, vmem_limit_bytes=VMEM_LIMIT)


def _rms_rows(x, g):
    return x * lax.rsqrt(jnp.mean(x * x, -1, keepdims=True) + EPS) * g


def _layernorm_rows(x, g, b):
    mu = jnp.mean(x, -1, keepdims=True)
    xc = x - mu
    var = jnp.mean(xc * xc, -1, keepdims=True)
    return xc * lax.rsqrt(var + EPS) * g + b


def _gelu(x):
    return 0.5 * x * (1.0 + lax.erf(x * 0.7071067811865476))


def _silu(x):
    return x * jax.nn.sigmoid(x)


def _dot(a, b):
    return jnp.dot(a, b, preferred_element_type=F32)


def _dot_hi(a, b):
    return jnp.dot(a, b, precision=HI, preferred_element_type=F32)


def _dot_nt_hi(a, b):
    return lax.dot_general(a, b, (((1,), (1,)), ((), ())), precision=HI, preferred_element_type=F32)


def _dot_tn_hi(a, b):
    return lax.dot_general(a, b, (((0,), (0,)), ((), ())), precision=HI, preferred_element_type=F32)


def _norm_matmul_kernel(x_ref, g_ref, w_ref, o_ref, h_ref):
    @pl.when(pl.program_id(1) == 0)
    def _():
        h_ref[...] = _rms_rows(x_ref[...], g_ref[...]).astype(BF16)

    o_ref[...] = _dot(h_ref[...], w_ref[...])


def _norm_matmul(x, g, w, n_out, tm, tn):
    m, d = x.shape
    return pl.pallas_call(
        _norm_matmul_kernel,
        grid=(m // tm, n_out // tn),
        in_specs=[
            pl.BlockSpec((tm, d), lambda i, j: (i, 0)),
            pl.BlockSpec((1, d), lambda i, j: (0, 0)),
            pl.BlockSpec((d, tn), lambda i, j: (0, j)),
        ],
        out_specs=pl.BlockSpec((tm, tn), lambda i, j: (i, j)),
        out_shape=jax.ShapeDtypeStruct((m, n_out), F32),
        scratch_shapes=[pltpu.VMEM((tm, d), BF16)],
        compiler_params=_params("parallel", "arbitrary"),
        name="norm_matmul",
    )(x, g.reshape(1, d), w)


def _mix_out_kernel(a0_ref, a1_ref, w0_ref, w1_ref, x_ref, o_ref):
    o_ref[...] = x_ref[...] + _dot(a0_ref[...], w0_ref[...]) + _dot(a1_ref[...], w1_ref[...])


def _mix_out(a0, a1, w, x, tm, tn):
    m, kh = a0.shape
    n = w.shape[1]
    return pl.pallas_call(
        _mix_out_kernel,
        grid=(m // tm, n // tn),
        in_specs=[
            pl.BlockSpec((tm, kh), lambda i, j: (i, 0)),
            pl.BlockSpec((tm, kh), lambda i, j: (i, 0)),
            pl.BlockSpec((kh, tn), lambda i, j: (0, j)),
            pl.BlockSpec((kh, tn), lambda i, j: (1, j)),
            pl.BlockSpec((tm, tn), lambda i, j: (i, j)),
        ],
        out_specs=pl.BlockSpec((tm, tn), lambda i, j: (i, j)),
        out_shape=jax.ShapeDtypeStruct((m, n), F32),
        compiler_params=_params("parallel", "arbitrary"),
        name="mix_out",
    )(a0, a1, w, w, x)


def _ffn_kernel(x_ref, g_ref, wg_ref, wu_ref, wd_ref, gf_ref, o_ref, h_ref, *, final_norm):
    f = pl.program_id(1)

    @pl.when(f == 0)
    def _():
        x = x_ref[...]
        h_ref[...] = _rms_rows(x, g_ref[...]).astype(BF16)
        o_ref[...] = x

    h = h_ref[...]
    act = _silu(_dot(h, wg_ref[...])) * _dot(h, wu_ref[...])
    o_ref[...] += _dot(act.astype(BF16), wd_ref[...])

    if final_norm:
        @pl.when(f == pl.num_programs(1) - 1)
        def _():
            o_ref[...] = _rms_rows(o_ref[...], gf_ref[...])


def _ffn(x, g, w_up, w_down, g_final, tm, tf):
    m, d = x.shape
    d_ff = w_down.shape[0]
    nf = d_ff // tf
    final_norm = g_final is not None
    gf = (g_final if final_norm else g).reshape(1, d)
    return pl.pallas_call(
        functools.partial(_ffn_kernel, final_norm=final_norm),
        grid=(m // tm, nf),
        in_specs=[
            pl.BlockSpec((tm, d), lambda i, f: (i, 0)),
            pl.BlockSpec((1, d), lambda i, f: (0, 0)),
            pl.BlockSpec((d, tf), lambda i, f: (0, f)),
            pl.BlockSpec((d, tf), lambda i, f: (0, f + nf)),
            pl.BlockSpec((tf, d), lambda i, f: (f, 0)),
            pl.BlockSpec((1, d), lambda i, f: (0, 0)),
        ],
        out_specs=pl.BlockSpec((tm, d), lambda i, f: (i, 0)),
        out_shape=jax.ShapeDtypeStruct((m, d), F32),
        scratch_shapes=[pltpu.VMEM((tm, d), BF16)],
        compiler_params=_params("parallel", "arbitrary"),
        name="ffn",
    )(x, g.reshape(1, d), w_up, w_up, w_down, gf)


POOL_HALO = 16


def _even_prompt_kernel(a_ref, halo_ref, u_ref, v_ref, pw_ref, ps_ref, lg_ref, lb_ref, ws_ref, bst_ref,
                        oa_ref, ob_ref, buf_ref):
    t = pl.program_id(1)
    tt, ch = a_ref.shape
    gch = ch // len(POOL_WINDOWS)
    a = a_ref[...]
    buf_ref[0:POOL_HALO, :] = jnp.where(t > 0, halo_ref[...], 0.0)
    buf_ref[POOL_HALO:, :] = a
    pos = t * tt + lax.broadcasted_iota(jnp.int32, (tt, 1), 0)
    for gi, w in enumerate(POOL_WINDOWS):
        cols = slice(gi * gch, (gi + 1) * gch)
        s = a[:, cols]
        for k in range(1, w):
            s = s + buf_ref[POOL_HALO - k:POOL_HALO - k + tt, cols]
        cnt = jnp.minimum(w, pos + 1).astype(F32)
        pooled = s / cnt - a[:, cols]
        mixed = _dot(pooled.astype(BF16), pw_ref[gi])
        oa_ref[:, cols] = (mixed * ps_ref[:, cols]).astype(BF16)

    u = _gelu(u_ref[...])
    vn = _layernorm_rows(_gelu(v_ref[...]), lg_ref[...], lb_ref[...])
    hch = ch // SGU_HEADS
    row = lax.broadcasted_iota(jnp.int32, (SGU_CHUNK, SGU_CHUNK), 0)
    col = lax.broadcasted_iota(jnp.int32, (SGU_CHUNK, SGU_CHUNK), 1)
    for h in range(SGU_HEADS):
        cols = slice(h * hch, (h + 1) * hch)
        wm = jnp.where(row >= col, ws_ref[h], 0.0).astype(BF16)
        bias = bst_ref[:, h:h + 1]
        for c in range(tt // SGU_CHUNK):
            rows = slice(c * SGU_CHUNK, (c + 1) * SGU_CHUNK)
            mixed = _dot(wm, vn[rows, cols].astype(BF16)) + bias
            ob_ref[rows, cols] = (u[rows, cols] * mixed).astype(BF16)


def _even_prompt(z3, pool_w, pool_scale, ln_g, ln_b, ws, bs, tt):
    b, t_len, _ = z3.shape
    ch = pool_scale.shape[0]
    hb = tt // POOL_HALO
    row = lambda v: v.reshape(1, ch)
    const = lambda *shape: pl.BlockSpec(shape, lambda i, t: (0,) * len(shape))
    out = jax.ShapeDtypeStruct((b, t_len, ch), BF16)
    return pl.pallas_call(
        _even_prompt_kernel,
        grid=(b, t_len // tt),
        in_specs=[
            pl.BlockSpec((None, tt, ch), lambda i, t: (i, t, 0)),
            pl.BlockSpec((None, POOL_HALO, ch), lambda i, t: (i, jnp.maximum(t * hb - 1, 0), 0)),
            pl.BlockSpec((None, tt, ch), lambda i, t: (i, t, 1)),
            pl.BlockSpec((None, tt, ch), lambda i, t: (i, t, 2)),
            const(*pool_w.shape), const(1, ch), const(1, ch), const(1, ch),
            const(*ws.shape), const(SGU_CHUNK, SGU_HEADS),
        ],
        out_specs=[pl.BlockSpec((None, tt, ch), lambda i, t: (i, t, 0))] * 2,
        out_shape=[out, out],
        scratch_shapes=[pltpu.VMEM((POOL_HALO + tt, ch), F32)],
        compiler_params=_params("parallel", "arbitrary"),
        name="even_prompt",
    )(z3, z3, z3, z3, pool_w.astype(BF16), row(pool_scale), row(ln_g), row(ln_b), ws, bs.T)


def _even_sample_kernel(a_ref, u_ref, v_ref, st_ref, pw_ref, ps_ref, lg_ref, lb_ref, w0_ref, b0_ref,
                        oa_ref, ob_ref, nst_ref, nv_ref):
    ch = a_ref.shape[1]
    gch = ch // len(POOL_WINDOWS)
    a = a_ref[...]
    for gi, w in enumerate(POOL_WINDOWS):
        cols = slice(gi * gch, (gi + 1) * gch)
        s = a[:, cols]
        for k in range(1, w):
            s = s + st_ref[:, POOL_HIST - k, cols]
        pooled = s / float(min(w, PAST_LEN + 1)) - a[:, cols]
        mixed = _dot(pooled.astype(BF16), pw_ref[gi])
        oa_ref[:, cols] = (mixed * ps_ref[:, cols]).astype(BF16)
    nst_ref[:, 0:POOL_HIST - 1, :] = st_ref[:, 1:POOL_HIST, :]
    nst_ref[:, POOL_HIST - 1, :] = a

    u = _gelu(u_ref[...])
    vn = _layernorm_rows(_gelu(v_ref[...]), lg_ref[...], lb_ref[...])
    nv_ref[...] = vn
    ob_ref[...] = (u * (vn * w0_ref[...] + b0_ref[...])).astype(BF16)


def _even_sample(z, state, pool_w, pool_scale, ln_g, ln_b, ws, bs, bt):
    assert PAST_LEN % SGU_CHUNK == 0
    n = z.shape[0]
    ch = pool_scale.shape[0]
    hch = ch // SGU_HEADS
    row = lambda v: v.reshape(1, ch)
    const = lambda *shape: pl.BlockSpec(shape, lambda i: (0,) * len(shape))
    w0 = jnp.repeat(ws[:, 0, 0], hch)
    b0 = jnp.repeat(bs[:, 0], hch)
    act = jax.ShapeDtypeStruct((n, ch), BF16)
    return pl.pallas_call(
        _even_sample_kernel,
        grid=(n // bt,),
        in_specs=[
            pl.BlockSpec((bt, ch), lambda i: (i, 0)),
            pl.BlockSpec((bt, ch), lambda i: (i, 1)),
            pl.BlockSpec((bt, ch), lambda i: (i, 2)),
            pl.BlockSpec((bt, POOL_HIST, ch), lambda i: (i, 0, 0)),
            const(*pool_w.shape), const(1, ch), const(1, ch), const(1, ch), const(1, ch), const(1, ch),
        ],
        out_specs=[
            pl.BlockSpec((bt, ch), lambda i: (i, 0)),
            pl.BlockSpec((bt, ch), lambda i: (i, 0)),
            pl.BlockSpec((bt, POOL_HIST, ch), lambda i: (i, 0, 0)),
            pl.BlockSpec((bt, ch), lambda i: (i, 0)),
        ],
        out_shape=[act, act, jax.ShapeDtypeStruct(state.shape, F32), jax.ShapeDtypeStruct((n, ch), F32)],
        compiler_params=_params("parallel"),
        name="even_sample",
    )(z, z, z, state, pool_w.astype(BF16), row(pool_scale), row(ln_g), row(ln_b), row(w0), row(b0))


CONV_HALO = 32


def _conv_prompt_kernel(p_ref, q_ref, hp_ref, hq_ref, dw_ref, db_ref, lg_ref, lb_ref, o_ref, st_ref, buf_ref):
    t = pl.program_id(1)
    tt = p_ref.shape[0]
    hist = CONV_WIDTH - 1
    glu = p_ref[...] * jax.nn.sigmoid(q_ref[...])
    halo = hp_ref[...] * jax.nn.sigmoid(hq_ref[...])
    buf_ref[0:CONV_HALO, :] = jnp.where(t > 0, halo, 0.0)
    buf_ref[CONV_HALO:, :] = glu
    base = CONV_HALO - hist
    acc = db_ref[...] + dw_ref[hist:hist + 1, :] * glu
    for k in range(hist):
        acc = acc + dw_ref[k:k + 1, :] * buf_ref[base + k:base + k + tt, :]
    o_ref[...] = _silu(_layernorm_rows(acc, lg_ref[...], lb_ref[...])).astype(BF16)

    @pl.when(t == pl.num_programs(1) - 1)
    def _():
        st_ref[...] = buf_ref[CONV_HALO + tt - hist:CONV_HALO + tt, :]


def _conv_prompt(z3, dw, db, ln_g, ln_b, tt):
    b, t_len, _ = z3.shape
    ch = db.shape[0]
    hb = tt // CONV_HALO
    hist = CONV_WIDTH - 1
    row = lambda v: v.reshape(1, ch)
    const = lambda *shape: pl.BlockSpec(shape, lambda i, t: (0,) * len(shape))
    halo = lambda c: pl.BlockSpec((None, CONV_HALO, ch), lambda i, t: (i, jnp.maximum(t * hb - 1, 0), c))
    return pl.pallas_call(
        _conv_prompt_kernel,
        grid=(b, t_len // tt),
        in_specs=[
            pl.BlockSpec((None, tt, ch), lambda i, t: (i, t, 0)),
            pl.BlockSpec((None, tt, ch), lambda i, t: (i, t, 1)),
            halo(0), halo(1),
            const(CONV_WIDTH, ch), const(1, ch), const(1, ch), const(1, ch),
        ],
        out_specs=[
            pl.BlockSpec((None, tt, ch), lambda i, t: (i, t, 0)),
            pl.BlockSpec((None, hist, ch), lambda i, t: (i, 0, 0)),
        ],
        out_shape=[jax.ShapeDtypeStruct((b, t_len, ch), BF16), jax.ShapeDtypeStruct((b, hist, ch), F32)],
        scratch_shapes=[pltpu.VMEM((CONV_HALO + tt, ch), F32)],
        compiler_params=_params("parallel", "arbitrary"),
        name="conv_prompt",
    )(z3, z3, z3, z3, dw, row(db), row(ln_g), row(ln_b))


DN_HALO = 8


def _l2norm_heads(x, scale):
    outs = []
    for h in range(x.shape[1] // DN_DK):
        xh = x[:, h * DN_DK:(h + 1) * DN_DK]
        outs.append(xh * (lax.rsqrt(jnp.sum(xh * xh, -1, keepdims=True) + EPS) * scale))
    return jnp.concatenate(outs, -1)


def _dn_prep_kernel(x_ref, halo_ref, w_ref, o_ref, buf_ref):
    t = pl.program_id(1)
    part = pl.program_id(2)
    tt = x_ref.shape[0]
    hist = DN_CONV - 1
    x = x_ref[...]
    buf_ref[0:DN_HALO, :] = jnp.where(t > 0, halo_ref[...], 0.0)
    buf_ref[DN_HALO:, :] = x
    base = DN_HALO - hist
    acc = w_ref[hist:hist + 1, :] * x
    for k in range(hist):
        acc = acc + w_ref[k:k + 1, :] * buf_ref[base + k:base + k + tt, :]
    y = _silu(acc)

    @pl.when(part == 0)
    def _():
        o_ref[...] = _l2norm_heads(y, DN_DK ** -0.5)

    @pl.when(part == 1)
    def _():
        o_ref[...] = _l2norm_heads(y, 1.0)

    @pl.when(part == 2)
    def _():
        o_ref[...] = y


def _dn_prep(z3, conv_w, col0, tt):
    b, t_len, _ = z3.shape
    ch = conv_w.shape[1] // 3
    hb = tt // DN_HALO
    return pl.pallas_call(
        _dn_prep_kernel,
        grid=(b, t_len // tt, 3),
        in_specs=[
            pl.BlockSpec((None, tt, ch), lambda i, t, p: (i, t, col0 + p)),
            pl.BlockSpec((None, DN_HALO, ch), lambda i, t, p: (i, jnp.maximum(t * hb - 1, 0), col0 + p)),
            pl.BlockSpec((DN_CONV, ch), lambda i, t, p: (0, p)),
        ],
        out_specs=pl.BlockSpec((None, tt, ch), lambda i, t, p: (i, t, p)),
        out_shape=jax.ShapeDtypeStruct((b, t_len, 3 * ch), F32),
        scratch_shapes=[pltpu.VMEM((DN_HALO + tt, ch), F32)],
        compiler_params=_params("parallel", "arbitrary", "arbitrary"),
        name="dn_prep",
    )(z3, z3, conv_w)


def _gate_terms(ab, alog, dtb):
    g = -jnp.exp(alog) * jax.nn.softplus(ab + dtb)
    return g, jax.nn.sigmoid(ab)


def _delta_prompt_kernel(q_ref, k_ref, v_ref, zg_ref, ab_ref, alog_ref, dtb_ref, ng_ref, o_ref, sfin_ref, s_ref):
    c = pl.program_id(1)
    n = q_ref.shape[0]

    @pl.when(c == 0)
    def _():
        s_ref[...] = jnp.zeros_like(s_ref)

    g, beta = _gate_terms(ab_ref[...], alog_ref[...], dtb_ref[...])
    row = lax.broadcasted_iota(jnp.int32, (n, n), 0)
    col = lax.broadcasted_iota(jnp.int32, (n, n), 1)
    lower = row >= col
    strict = row > col
    eye = (row == col).astype(F32)
    same_block = (row // DN_SOLVE_BLOCK) == (col // DN_SOLVE_BLOCK)
    tri = lower.astype(F32)
    gcol = _dot_hi(tri, g)
    grow = _dot_nt_hi(g.T, tri)

    for h in range(DN_HEADS):
        cols = slice(h * DN_DK, (h + 1) * DN_DK)
        q = q_ref[:, cols]
        k = k_ref[:, cols]
        v = v_ref[:, cols]
        gi = gcol[:, h:h + 1]
        gj = grow[h:h + 1, :]
        bi = beta[:, DN_HEADS + h:DN_HEADS + h + 1]
        decay = jnp.where(lower, jnp.exp(jnp.where(lower, gi - gj, 0.0)), 0.0)
        eg = jnp.exp(gi)
        kb = k * bi
        amat = jnp.where(strict, _dot_nt_hi(kb, k) * decay, 0.0)
        ad = jnp.where(same_block, amat, 0.0)
        an = amat - ad
        dinv = eye - ad
        pw = ad
        for _ in range(DN_SOLVE_BLOCK.bit_length() - 2):
            pw = _dot_hi(pw, pw)
            dinv = _dot_hi(dinv, eye + pw)
        bm = _dot_hi(dinv, an)
        sol = _dot_hi(dinv, jnp.concatenate([v * bi, kb * eg], -1))
        assert n // DN_SOLVE_BLOCK == 4
        b2 = _dot_hi(bm, bm)
        sol = sol + _dot_hi(b2, sol)
        sol = sol - _dot_hi(bm, sol)
        u = sol[:, :DN_DK]
        w = sol[:, DN_DK:]
        qk = _dot_nt_hi(q, k) * decay
        s = s_ref[h]
        v_new = u - _dot_hi(w, s)
        o = _dot_hi(q * eg, s) + _dot_hi(qk, v_new)
        g_last = gcol[n - 1:n, h:h + 1]
        s_ref[h] = s * jnp.exp(g_last) + _dot_tn_hi(k * jnp.exp(g_last - gi), v_new)
        o = o * lax.rsqrt(jnp.mean(o * o, -1, keepdims=True) + EPS) * ng_ref[...]
        o_ref[:, cols] = (o * _silu(zg_ref[:, cols])).astype(BF16)

    @pl.when(c == pl.num_programs(1) - 1)
    def _():
        sfin_ref[...] = s_ref[...]


def _delta_prompt(qkv3, z3, zg_col, ab3, a_log, dt_bias, norm_g):
    b, t_len, _ = qkv3.shape
    ch = DN_HEADS * DN_DK
    n = DN_CHUNK
    const = lambda *shape: pl.BlockSpec(shape, lambda i, c: (0,) * len(shape))
    return pl.pallas_call(
        _delta_prompt_kernel,
        grid=(b, t_len // n),
        in_specs=[
            pl.BlockSpec((None, n, ch), lambda i, c: (i, c, 0)),
            pl.BlockSpec((None, n, ch), lambda i, c: (i, c, 1)),
            pl.BlockSpec((None, n, ch), lambda i, c: (i, c, 2)),
            pl.BlockSpec((None, n, ch), lambda i, c: (i, c, zg_col)),
            pl.BlockSpec((None, n, LANES), lambda i, c: (i, c, 0)),
            const(1, LANES), const(1, LANES), const(1, DN_DK),
        ],
        out_specs=[
            pl.BlockSpec((None, n, ch), lambda i, c: (i, c, 0)),
            pl.BlockSpec((None, DN_HEADS, DN_DK, DN_DK), lambda i, c: (i, 0, 0, 0)),
        ],
        out_shape=[jax.ShapeDtypeStruct((b, t_len, ch), BF16),
                   jax.ShapeDtypeStruct((b, DN_HEADS, DN_DK, DN_DK), F32)],
        scratch_shapes=[pltpu.VMEM((DN_HEADS, DN_DK, DN_DK), F32)],
        compiler_params=_params("parallel", "arbitrary"),
        name="delta_prompt",
    )(qkv3, qkv3, qkv3, z3, ab3, _lane_row(a_log), _lane_row(dt_bias), norm_g.reshape(1, DN_DK))


def _lane_row(v):
    return jnp.pad(v, (0, LANES - v.shape[0])).reshape(1, LANES)


def _odd_sample_kernel(p_ref, gt_ref, q_ref, k_ref, v_ref, zg_ref, ab_ref, cst_ref, dst_ref, s_ref,
                       dw_ref, db_ref, lg_ref, lb_ref, cw_ref, alog_ref, dtb_ref, ng_ref,
                       oc_ref, od_ref, ncst_ref, ndst_ref, ns_ref, obuf_ref):
    bt, ch = p_ref.shape
    hist = CONV_WIDTH - 1
    glu = p_ref[...] * jax.nn.sigmoid(gt_ref[...])
    acc = db_ref[...] + dw_ref[hist:hist + 1, :] * glu
    for k in range(hist):
        acc = acc + dw_ref[k:k + 1, :] * cst_ref[:, k, :]
    oc_ref[...] = _silu(_layernorm_rows(acc, lg_ref[...], lb_ref[...])).astype(BF16)
    ncst_ref[:, 0:hist - 1, :] = cst_ref[:, 1:hist, :]
    ncst_ref[:, hist - 1, :] = glu

    dh = DN_CONV - 1
    parts = []
    for pi, x_ref in enumerate((q_ref, k_ref, v_ref)):
        cols = slice(pi * ch, (pi + 1) * ch)
        x = x_ref[...]
        y = cw_ref[dh:dh + 1, cols] * x
        for k in range(dh):
            y = y + cw_ref[k:k + 1, cols] * dst_ref[:, k, cols]
        parts.append(_silu(y))
        ndst_ref[:, 0:dh - 1, cols] = dst_ref[:, 1:dh, cols]
        ndst_ref[:, dh - 1, cols] = x
    q = _l2norm_heads(parts[0], DN_DK ** -0.5)
    k = _l2norm_heads(parts[1], 1.0)
    v = parts[2]
    g, beta = _gate_terms(ab_ref[...], alog_ref[...], dtb_ref[...])
    eg_all = jnp.exp(g)

    assert 2 * DN_HEADS * bt == LANES
    kq = jnp.concatenate([x[:, h * DN_DK:(h + 1) * DN_DK] for x in (k, q) for h in range(DN_HEADS)], 0)
    kq_t = kq.T
    for h in range(DN_HEADS):
        cols = slice(h * DN_DK, (h + 1) * DN_DK)
        for b in range(bt):
            kc = kq_t[:, h * bt + b:h * bt + b + 1]
            qc = kq_t[:, (DN_HEADS + h) * bt + b:(DN_HEADS + h) * bt + b + 1]
            s = s_ref[b, h]
            sk = jnp.sum(s * kc, 0, keepdims=True)
            sq = jnp.sum(s * qc, 0, keepdims=True)
            eg = eg_all[b:b + 1, h:h + 1]
            bi = beta[b:b + 1, DN_HEADS + h:DN_HEADS + h + 1]
            v_new = bi * (v[b:b + 1, cols] - eg * sk)
            qk = jnp.sum(q[b:b + 1, cols] * k[b:b + 1, cols], -1, keepdims=True)
            obuf_ref[b:b + 1, cols] = eg * sq + qk * v_new
            ns_ref[b, h] = s * eg + kc * v_new
    for h in range(DN_HEADS):
        cols = slice(h * DN_DK, (h + 1) * DN_DK)
        o = obuf_ref[:, cols]
        o = o * lax.rsqrt(jnp.mean(o * o, -1, keepdims=True) + EPS) * ng_ref[...]
        od_ref[:, cols] = (o * _silu(zg_ref[:, cols])).astype(BF16)


def _odd_sample(z, ab, cstate, dstate, sstate, dw, db, ln_g, ln_b, conv_w, a_log, dt_bias, norm_g, bt):
    n = z.shape[0]
    ch = db.shape[0]
    row = lambda v: v.reshape(1, ch)
    const = lambda *shape: pl.BlockSpec(shape, lambda i: (0,) * len(shape))
    zcol = lambda c: pl.BlockSpec((bt, ch), lambda i: (i, c))
    lead = lambda a: pl.BlockSpec((bt,) + a.shape[1:], lambda i: (i,) + (0,) * (a.ndim - 1))
    act = jax.ShapeDtypeStruct((n, ch), BF16)
    same = lambda a: jax.ShapeDtypeStruct(a.shape, F32)
    return pl.pallas_call(
        _odd_sample_kernel,
        grid=(n // bt,),
        in_specs=[zcol(0), zcol(1), zcol(2), zcol(3), zcol(4), zcol(5),
                  pl.BlockSpec((bt, LANES), lambda i: (i, 0)),
                  lead(cstate), lead(dstate), lead(sstate),
                  const(CONV_WIDTH, ch), const(1, ch), const(1, ch), const(1, ch),
                  const(DN_CONV, 3 * ch), const(1, LANES), const(1, LANES), const(1, DN_DK)],
        out_specs=[pl.BlockSpec((bt, ch), lambda i: (i, 0)), pl.BlockSpec((bt, ch), lambda i: (i, 0)),
                   lead(cstate), lead(dstate), lead(sstate)],
        out_shape=[act, act, same(cstate), same(dstate), same(sstate)],
        scratch_shapes=[pltpu.VMEM((bt, ch), F32)],
        compiler_params=_params("parallel"),
        name="odd_sample",
    )(z, z, z, z, z, z, ab, cstate, dstate, sstate, dw, row(db), row(ln_g), row(ln_b), conv_w,
      _lane_row(a_log), _lane_row(dt_bias), norm_g.reshape(1, DN_DK))


def kernel(x_prompt, x_sample, state_pool, state_conv_c, state_dn_conv, state_dn_S, norm_mix, norm_ffn, norm_final, ev_w_in, pool_w, pool_scale, sgu_ln_g, sgu_ln_b, sgu_ws, sgu_b, ev_w_out, od_w_in, cv_dw, cv_db, cv_ln_g, cv_ln_b, dn_conv_w, dn_a_log, dn_dt_bias, dn_norm_g, od_w_out, ffn_w_up, ffn_w_down):
    bp, t_len, d = x_prompt.shape
    bs = x_sample.shape[0]
    assert x_sample.shape[1] == 1
    depth = norm_mix.shape[0]
    ch = pool_scale.shape[1]
    odd_main = 6 * ch
    assert od_w_in.shape[2] == odd_main + 2 * DN_HEADS

    ev_in = ev_w_in.astype(BF16)
    ev_out = ev_w_out.astype(BF16)
    od_in = od_w_in.astype(BF16)
    od_ab = jnp.pad(od_in[:, :, odd_main:], ((0, 0), (0, 0), (0, LANES - 2 * DN_HEADS)))
    od_out = od_w_out.astype(BF16)
    w_up = ffn_w_up.astype(BF16)
    w_down = ffn_w_down.astype(BF16)

    xp = x_prompt.reshape(bp * t_len, d)
    xs = x_sample.reshape(bs, d)
    tm_p, tm_s, tn, tf = 512, bs, 512, 512
    pool_p, pool_s, v_s, cc_p, cc_s, dc_p, dc_s, s_p, s_s = [], [], [], [], [], [], [], [], []
    for l in range(depth):
        i = l // 2
        if l % 2 == 0:
            zp = _norm_matmul(xp, norm_mix[l], ev_in[i], 3 * ch, tm_p, tn)
            zs = _norm_matmul(xs, norm_mix[l], ev_in[i], 3 * ch, tm_s, tn)
            zp3 = zp.reshape(bp, t_len, 3 * ch)
            a_p, b_p = _even_prompt(zp3, pool_w[i], pool_scale[i], sgu_ln_g[i], sgu_ln_b[i], sgu_ws[i], sgu_b[i], 256)
            a_s, b_s, npool, nv = _even_sample(zs, state_pool[i], pool_w[i], pool_scale[i], sgu_ln_g[i], sgu_ln_b[i],
                                               sgu_ws[i], sgu_b[i], 8)
            pool_p.append(zp3[:, t_len - POOL_HIST:, :ch])
            pool_s.append(npool)
            v_s.append(nv.reshape(bs, 1, ch))
            xp = _mix_out(a_p.reshape(bp * t_len, ch), b_p.reshape(bp * t_len, ch), ev_out[i], xp, tm_p, tn)
            xs = _mix_out(a_s, b_s, ev_out[i], xs, tm_s, tn)
        else:
            zp = _norm_matmul(xp, norm_mix[l], od_in[i], odd_main, tm_p, tn)
            zs = _norm_matmul(xs, norm_mix[l], od_in[i], odd_main, tm_s, tn)
            abp = _norm_matmul(xp, norm_mix[l], od_ab[i], LANES, tm_p, LANES)
            abs_ = _norm_matmul(xs, norm_mix[l], od_ab[i], LANES, tm_s, LANES)
            zp3 = zp.reshape(bp, t_len, odd_main)
            c_p, ncc = _conv_prompt(zp3, cv_dw[i], cv_db[i], cv_ln_g[i], cv_ln_b[i], 256)
            qkv3 = _dn_prep(zp3, dn_conv_w[i], 2, 256)
            d_p, s_fin = _delta_prompt(qkv3, zp3, 5, abp.reshape(bp, t_len, LANES), dn_a_log[i], dn_dt_bias[i],
                                       dn_norm_g[i])
            c_s, d_s, ncc_s, ndc_s, ns_s = _odd_sample(zs, abs_, state_conv_c[i], state_dn_conv[i], state_dn_S[i],
                                                       cv_dw[i], cv_db[i], cv_ln_g[i], cv_ln_b[i], dn_conv_w[i],
                                                       dn_a_log[i], dn_dt_bias[i], dn_norm_g[i], 8)
            cc_p.append(ncc)
            cc_s.append(ncc_s)
            dc_p.append(zp3[:, t_len - (DN_CONV - 1):, 2 * ch:5 * ch])
            dc_s.append(ndc_s)
            s_p.append(s_fin)
            s_s.append(ns_s)
            xp = _mix_out(c_p.reshape(bp * t_len, ch), d_p.reshape(bp * t_len, ch), od_out[i], xp, tm_p, tn)
            xs = _mix_out(c_s, d_s, od_out[i], xs, tm_s, tn)
        gf = norm_final if l == depth - 1 else None
        xp = _ffn(xp, norm_ffn[l], w_up[l], w_down[l], gf, tm_p, tf)
        xs = _ffn(xs, norm_ffn[l], w_up[l], w_down[l], gf, tm_s, tf)
    st = jnp.stack
    return (xp.reshape(bp, t_len, d), xs.reshape(bs, 1, d), st(pool_p), st(pool_s), st(v_s), st(cc_p), st(cc_s),
            st(dc_p), st(dc_s), st(s_p), st(s_s))
```

```python
import functools

import jax
import jax.numpy as jnp
from jax import lax
from jax.experimental import pallas as pl
from jax.experimental.pallas import tpu as pltpu

F32 = jnp.float32
BF16 = jnp.bfloat16
HI = lax.Precision.HIGHEST

EPS = 1e-6
PAST_LEN = 16384
POOL_WINDOWS = (2, 4, 8, 16)
POOL_HIST = max(POOL_WINDOWS) - 1
SGU_HEADS = 4
SGU_CHUNK = 128
CONV_WIDTH = 31
DN_HEADS = 8
DN_DK = 128
DN_CONV = 4
DN_CHUNK = 64
DN_SOLVE_BLOCK = 16
DN_STEP_CHUNKS = 2

LANES = 128
VMEM_LIMIT = 48 * 1024 * 1024

POOL_HALO = 16
CONV_HALO = 32
DN_HALO = 8


def _tiles(m):
    return min(m, 512), 512, 512


def _params(*semantics):
    return pltpu.CompilerParams(dimension_semantics=semantics, vmem_limit_bytes=VMEM_LIMIT)


def _layer(lyr, block, index_map):
    return pl.BlockSpec((None,) + tuple(block), lambda *g: (lyr,) + tuple(index_map(*g)))


def _whole(lyr, arr):
    return pl.BlockSpec((None,) + arr.shape[1:], lambda *g: (lyr,) + (0,) * (arr.ndim - 1))


def _skip_refs(fn, start, count):
    def body(*refs):
        return fn(*refs[:start], *refs[start + count:])
    return body


def _own_layer_zero_rest(fn, n_in, outs, lyr):
    def body(*refs):
        refs = list(refs)
        for o in outs:
            full = refs[n_in + o]
            for other in range(full.shape[0]):
                if other != lyr:
                    full[other] = jnp.zeros(full.shape[1:], full.dtype)
            refs[n_in + o] = full.at[lyr]
        return fn(*refs)
    return body


def _stacked_call(kernel_fn, n_in, carried, stacked_out, lyr, **kw):
    out_specs = list(kw.pop("out_specs"))
    if carried is None:
        for o, (block, imap) in stacked_out.items():
            n_lyr = kw["out_shape"][o].shape[0]
            out_specs[o] = pl.BlockSpec((n_lyr,) + tuple(block), lambda *g, imap=imap: (0,) + tuple(imap(*g)))
        body = _own_layer_zero_rest(kernel_fn, n_in, tuple(stacked_out), lyr)
        return pl.pallas_call(body, out_specs=out_specs, **kw), ()
    for o, (block, imap) in stacked_out.items():
        out_specs[o] = _layer(lyr, block, imap)
    kw["in_specs"] = list(kw["in_specs"]) + [pl.BlockSpec(memory_space=pl.ANY)] * len(carried)
    aliases = {n_in + k: o for k, o in enumerate(stacked_out)}
    body = _skip_refs(kernel_fn, n_in, len(carried))
    return pl.pallas_call(body, out_specs=out_specs, input_output_aliases=aliases, **kw), tuple(carried)


def _rms_rows(x, g):
    return x * lax.rsqrt(jnp.mean(x * x, -1, keepdims=True) + EPS) * g


def _layernorm_rows(x, g, b):
    mu = jnp.mean(x, -1, keepdims=True)
    xc = x - mu
    var = jnp.mean(xc * xc, -1, keepdims=True)
    return xc * lax.rsqrt(var + EPS) * g + b


def _gelu(x):
    return 0.5 * x * (1.0 + lax.erf(x * 0.7071067811865476))


def _silu(x):
    return x * jax.nn.sigmoid(x)


def _dot(a, b):
    return jnp.dot(a, b, preferred_element_type=F32)


def _dot_hi(a, b):
    return jnp.dot(a, b, precision=HI, preferred_element_type=F32)


def _dot_nt_hi(a, b):
    return lax.dot_general(a, b, (((1,), (1,)), ((), ())), precision=HI, preferred_element_type=F32)


def _bdot(a, b):
    return jnp.dot(a.astype(BF16), b.astype(BF16), preferred_element_type=F32)


def _bdot_nt(a, b):
    return lax.dot_general(a.astype(BF16), b.astype(BF16), (((1,), (1,)), ((), ())), preferred_element_type=F32)


def _bdot_tn(a, b):
    return lax.dot_general(a.astype(BF16), b.astype(BF16), (((0,), (0,)), ((), ())), preferred_element_type=F32)


def _norm_matmul_kernel(x_ref, g_ref, w_ref, o_ref, h_ref):
    @pl.when(pl.program_id(1) == 0)
    def _():
        h_ref[...] = _rms_rows(x_ref[...], g_ref[...]).astype(BF16)

    o_ref[...] = _dot(h_ref[...], w_ref[...])


def _norm_matmul(x, gains, lyr, w, wl, n_out, tn):
    m, d = x.shape
    tm = _tiles(m)[0]
    return pl.pallas_call(
        _norm_matmul_kernel,
        grid=(m // tm, n_out // tn),
        in_specs=[
            pl.BlockSpec((tm, d), lambda i, j: (i, 0)),
            _whole(lyr, gains),
            _layer(wl, (d, tn), lambda i, j: (0, j)),
        ],
        out_specs=pl.BlockSpec((tm, tn), lambda i, j: (i, j)),
        out_shape=jax.ShapeDtypeStruct((m, n_out), F32),
        scratch_shapes=[pltpu.VMEM((tm, d), BF16)],
        compiler_params=_params("parallel", "arbitrary"),
        name="norm_matmul",
    )(x, gains, w)


def _mix_out_kernel(a0_ref, a1_ref, w0_ref, w1_ref, x_ref, o_ref):
    o_ref[...] = x_ref[...] + _dot(a0_ref[...], w0_ref[...]) + _dot(a1_ref[...], w1_ref[...])


def _mix_out(a0, a1, w, wl, x):
    m, kh = a0.shape
    n = w.shape[2]
    tm, tn, _ = _tiles(m)
    return pl.pallas_call(
        _mix_out_kernel,
        grid=(m // tm, n // tn),
        in_specs=[
            pl.BlockSpec((tm, kh), lambda i, j: (i, 0)),
            pl.BlockSpec((tm, kh), lambda i, j: (i, 0)),
            _layer(wl, (kh, tn), lambda i, j: (0, j)),
            _layer(wl, (kh, tn), lambda i, j: (1, j)),
            pl.BlockSpec((tm, tn), lambda i, j: (i, j)),
        ],
        out_specs=pl.BlockSpec((tm, tn), lambda i, j: (i, j)),
        out_shape=jax.ShapeDtypeStruct((m, n), F32),
        compiler_params=_params("parallel", "arbitrary"),
        name="mix_out",
    )(a0, a1, w, w, x)


def _ffn_kernel(x_ref, g_ref, wg_ref, wu_ref, wd_ref, gf_ref, o_ref, h_ref, *, final_norm):
    f = pl.program_id(1)

    @pl.when(f == 0)
    def _():
        x = x_ref[...]
        h_ref[...] = _rms_rows(x, g_ref[...]).astype(BF16)
        o_ref[...] = x

    h = h_ref[...]
    act = _silu(_dot(h, wg_ref[...])) * _dot(h, wu_ref[...])
    o_ref[...] += _dot(act.astype(BF16), wd_ref[...])

    if final_norm:
        @pl.when(f == pl.num_programs(1) - 1)
        def _():
            o_ref[...] = _rms_rows(o_ref[...], gf_ref[...])


def _ffn(x, gains, lyr, w_up, w_down, g_final, final_norm):
    m, d = x.shape
    d_ff = w_down.shape[1]
    tm, _, tf = _tiles(m)
    nf = d_ff // tf
    return pl.pallas_call(
        functools.partial(_ffn_kernel, final_norm=final_norm),
        grid=(m // tm, nf),
        in_specs=[
            pl.BlockSpec((tm, d), lambda i, f: (i, 0)),
            _whole(lyr, gains),
            _layer(lyr, (d, tf), lambda i, f: (0, f)),
            _layer(lyr, (d, tf), lambda i, f: (0, f + nf)),
            _layer(lyr, (tf, d), lambda i, f: (f, 0)),
            pl.BlockSpec((1, d), lambda i, f: (0, 0)),
        ],
        out_specs=pl.BlockSpec((tm, d), lambda i, f: (i, 0)),
        out_shape=jax.ShapeDtypeStruct((m, d), F32),
        scratch_shapes=[pltpu.VMEM((tm, d), BF16)],
        compiler_params=_params("parallel", "arbitrary"),
        name="ffn",
    )(x, gains, w_up, w_up, w_down, g_final)


def _even_prompt_kernel(a_ref, halo_ref, u_ref, v_ref, pw_ref, ps_ref, lg_ref, lb_ref, ws_ref, bst_ref,
                        oa_ref, ob_ref, buf_ref):
    t = pl.program_id(1)
    tt, ch = a_ref.shape
    gch = ch // len(POOL_WINDOWS)
    a = a_ref[...]
    buf_ref[0:POOL_HALO, :] = jnp.where(t > 0, halo_ref[...], 0.0)
    buf_ref[POOL_HALO:, :] = a
    pos = t * tt + lax.broadcasted_iota(jnp.int32, (tt, 1), 0)
    for gi, w in enumerate(POOL_WINDOWS):
        cols = slice(gi * gch, (gi + 1) * gch)
        s = a[:, cols]
        for k in range(1, w):
            s = s + buf_ref[POOL_HALO - k:POOL_HALO - k + tt, cols]
        cnt = jnp.minimum(w, pos + 1).astype(F32)
        pooled = s / cnt - a[:, cols]
        mixed = _dot(pooled.astype(BF16), pw_ref[gi])
        oa_ref[:, cols] = (mixed * ps_ref[:, cols]).astype(BF16)

    u = _gelu(u_ref[...])
    vn = _layernorm_rows(_gelu(v_ref[...]), lg_ref[...], lb_ref[...])
    hch = ch // SGU_HEADS
    row = lax.broadcasted_iota(jnp.int32, (SGU_CHUNK, SGU_CHUNK), 0)
    col = lax.broadcasted_iota(jnp.int32, (SGU_CHUNK, SGU_CHUNK), 1)
    for h in range(SGU_HEADS):
        cols = slice(h * hch, (h + 1) * hch)
        wm = jnp.where(row >= col, ws_ref[h], 0.0).astype(BF16)
        bias = bst_ref[:, h:h + 1]
        for c in range(tt // SGU_CHUNK):
            rows = slice(c * SGU_CHUNK, (c + 1) * SGU_CHUNK)
            mixed = _dot(wm, vn[rows, cols].astype(BF16)) + bias
            ob_ref[rows, cols] = (u[rows, cols] * mixed).astype(BF16)


def _even_prompt(z3, lyr, pool_w, pool_scale, ln_g, ln_b, ws, bst, tt):
    b, t_len, _ = z3.shape
    ch = pool_scale.shape[-1]
    hb = tt // POOL_HALO
    out = jax.ShapeDtypeStruct((b, t_len, ch), BF16)
    return pl.pallas_call(
        _even_prompt_kernel,
        grid=(b, t_len // tt),
        in_specs=[
            pl.BlockSpec((None, tt, ch), lambda i, t: (i, t, 0)),
            pl.BlockSpec((None, POOL_HALO, ch), lambda i, t: (i, jnp.maximum(t * hb - 1, 0), 0)),
            pl.BlockSpec((None, tt, ch), lambda i, t: (i, t, 1)),
            pl.BlockSpec((None, tt, ch), lambda i, t: (i, t, 2)),
            _whole(lyr, pool_w), _whole(lyr, pool_scale), _whole(lyr, ln_g), _whole(lyr, ln_b),
            _whole(lyr, ws), _whole(lyr, bst),
        ],
        out_specs=[pl.BlockSpec((None, tt, ch), lambda i, t: (i, t, 0))] * 2,
        out_shape=[out, out],
        scratch_shapes=[pltpu.VMEM((POOL_HALO + tt, ch), F32)],
        compiler_params=_params("parallel", "arbitrary"),
        name="even_prompt",
    )(z3, z3, z3, z3, pool_w, pool_scale, ln_g, ln_b, ws, bst)


def _even_sample_kernel(a_ref, u_ref, v_ref, st_ref, pw_ref, ps_ref, lg_ref, lb_ref, w0_ref, b0_ref,
                        oa_ref, ob_ref, nst_ref, nv_ref):
    ch = a_ref.shape[1]
    gch = ch // len(POOL_WINDOWS)
    a = a_ref[...]
    for gi, w in enumerate(POOL_WINDOWS):
        cols = slice(gi * gch, (gi + 1) * gch)
        s = a[:, cols]
        for k in range(1, w):
            s = s + st_ref[POOL_HIST - k, :, cols]
        pooled = s / float(min(w, PAST_LEN + 1)) - a[:, cols]
        mixed = _dot(pooled.astype(BF16), pw_ref[gi])
        oa_ref[:, cols] = (mixed * ps_ref[:, cols]).astype(BF16)
    nst_ref[0:POOL_HIST - 1] = st_ref[1:POOL_HIST]
    nst_ref[POOL_HIST - 1] = a

    u = _gelu(u_ref[...])
    vn = _layernorm_rows(_gelu(v_ref[...]), lg_ref[...], lb_ref[...])
    nv_ref[...] = vn
    ob_ref[...] = (u * (vn * w0_ref[...] + b0_ref[...])).astype(BF16)


def _even_sample(z, lyr, state, pool_w, pool_scale, ln_g, ln_b, w0, b0, carried, bt):
    assert PAST_LEN % SGU_CHUNK == 0
    n = z.shape[0]
    ch = pool_scale.shape[-1]
    act = jax.ShapeDtypeStruct((n, ch), BF16)
    rows = pl.BlockSpec((bt, ch), lambda i: (i, 0))
    n_in = 10
    stacked = {2: ((POOL_HIST, bt, ch), lambda i: (0, i, 0)), 3: ((bt, ch), lambda i: (i, 0))}
    call, extra = _stacked_call(
        _even_sample_kernel, n_in, carried, stacked, lyr,
        grid=(n // bt,),
        in_specs=[
            rows, pl.BlockSpec((bt, ch), lambda i: (i, 1)), pl.BlockSpec((bt, ch), lambda i: (i, 2)),
            _layer(lyr, (POOL_HIST, bt, ch), lambda i: (0, i, 0)),
            _whole(lyr, pool_w), _whole(lyr, pool_scale), _whole(lyr, ln_g), _whole(lyr, ln_b),
            _whole(lyr, w0), _whole(lyr, b0),
        ],
        out_specs=[rows, rows, None, None],
        out_shape=[act, act, jax.ShapeDtypeStruct(state.shape, F32),
                   jax.ShapeDtypeStruct((state.shape[0], n, ch), F32)],
        compiler_params=_params("parallel"),
        name="even_sample",
    )
    return call(z, z, z, state, pool_w, pool_scale, ln_g, ln_b, w0, b0, *extra)


def _conv_prompt_kernel(p_ref, q_ref, hp_ref, hq_ref, dw_ref, db_ref, lg_ref, lb_ref, o_ref, st_ref, buf_ref):
    t = pl.program_id(1)
    tt = p_ref.shape[0]
    hist = CONV_WIDTH - 1
    glu = p_ref[...] * jax.nn.sigmoid(q_ref[...])
    halo = hp_ref[...] * jax.nn.sigmoid(hq_ref[...])
    buf_ref[0:CONV_HALO, :] = jnp.where(t > 0, halo, 0.0)
    buf_ref[CONV_HALO:, :] = glu
    base = CONV_HALO - hist
    acc = db_ref[...] + dw_ref[hist:hist + 1, :] * glu
    for k in range(hist):
        acc = acc + dw_ref[k:k + 1, :] * buf_ref[base + k:base + k + tt, :]
    o_ref[...] = _silu(_layernorm_rows(acc, lg_ref[...], lb_ref[...])).astype(BF16)

    @pl.when(t == pl.num_programs(1) - 1)
    def _():
        st_ref[...] = buf_ref[CONV_HALO + tt - hist:CONV_HALO + tt, :]


def _conv_prompt(z3, lyr, n_lyr, dw, db, ln_g, ln_b, carried, tt):
    b, t_len, _ = z3.shape
    ch = db.shape[-1]
    hb = tt // CONV_HALO
    hist = CONV_WIDTH - 1
    halo = lambda c: pl.BlockSpec((None, CONV_HALO, ch), lambda i, t: (i, jnp.maximum(t * hb - 1, 0), c))
    stacked = {1: ((None, hist, ch), lambda i, t: (i, 0, 0))}
    call, extra = _stacked_call(
        _conv_prompt_kernel, 8, carried, stacked, lyr,
        grid=(b, t_len // tt),
        in_specs=[
            pl.BlockSpec((None, tt, ch), lambda i, t: (i, t, 0)),
            pl.BlockSpec((None, tt, ch), lambda i, t: (i, t, 1)),
            halo(0), halo(1),
            _whole(lyr, dw), _whole(lyr, db), _whole(lyr, ln_g), _whole(lyr, ln_b),
        ],
        out_specs=[
            pl.BlockSpec((None, tt, ch), lambda i, t: (i, t, 0)),
            None,
        ],
        out_shape=[jax.ShapeDtypeStruct((b, t_len, ch), BF16), jax.ShapeDtypeStruct((n_lyr, b, hist, ch), F32)],
        scratch_shapes=[pltpu.VMEM((CONV_HALO + tt, ch), F32)],
        compiler_params=_params("parallel", "arbitrary"),
        name="conv_prompt",
    )
    return call(z3, z3, z3, z3, dw, db, ln_g, ln_b, *extra)


def _l2norm_heads(x, scale):
    outs = []
    for h in range(x.shape[1] // DN_DK):
        xh = x[:, h * DN_DK:(h + 1) * DN_DK]
        outs.append(xh * (lax.rsqrt(jnp.sum(xh * xh, -1, keepdims=True) + EPS) * scale))
    return jnp.concatenate(outs, -1)


def _dn_prep_kernel(x_ref, halo_ref, w_ref, o_ref, buf_ref):
    t = pl.program_id(1)
    part = pl.program_id(2)
    tt = x_ref.shape[0]
    hist = DN_CONV - 1
    x = x_ref[...]
    buf_ref[0:DN_HALO, :] = jnp.where(t > 0, halo_ref[...], 0.0)
    buf_ref[DN_HALO:, :] = x
    base = DN_HALO - hist
    acc = w_ref[hist:hist + 1, :] * x
    for k in range(hist):
        acc = acc + w_ref[k:k + 1, :] * buf_ref[base + k:base + k + tt, :]
    y = _silu(acc)

    @pl.when(part == 0)
    def _():
        o_ref[...] = _l2norm_heads(y, DN_DK ** -0.5)

    @pl.when(part == 1)
    def _():
        o_ref[...] = _l2norm_heads(y, 1.0)

    @pl.when(part == 2)
    def _():
        o_ref[...] = y


def _dn_prep(z3, lyr, conv_w, col0, tt):
    b, t_len, _ = z3.shape
    ch = conv_w.shape[-1] // 3
    hb = tt // DN_HALO
    return pl.pallas_call(
        _dn_prep_kernel,
        grid=(b, t_len // tt, 3),
        in_specs=[
            pl.BlockSpec((None, tt, ch), lambda i, t, p: (i, t, col0 + p)),
            pl.BlockSpec((None, DN_HALO, ch), lambda i, t, p: (i, jnp.maximum(t * hb - 1, 0), col0 + p)),
            _layer(lyr, (DN_CONV, ch), lambda i, t, p: (0, p)),
        ],
        out_specs=pl.BlockSpec((None, tt, ch), lambda i, t, p: (i, t, p)),
        out_shape=jax.ShapeDtypeStruct((b, t_len, 3 * ch), F32),
        scratch_shapes=[pltpu.VMEM((DN_HALO + tt, ch), F32)],
        compiler_params=_params("parallel", "arbitrary", "arbitrary"),
        name="dn_prep",
    )(z3, z3, conv_w)


def _gate_terms(ab, alog, dtb):
    g = -jnp.exp(alog) * jax.nn.softplus(ab + dtb)
    return g, jax.nn.sigmoid(ab)


def _each(fn, *lists):
    return [fn(*xs) for xs in zip(*lists)]


def _delta_chunk_terms(q, k, v, gi, gj, bi, masks):
    lower, strict, same_block = masks
    n = q[0].shape[0]
    decay = _each(lambda a, b: jnp.where(lower, jnp.exp(jnp.where(lower, a - b, 0.0)), 0.0), gi, gj)
    eg = _each(jnp.exp, gi)
    kb = _each(jnp.multiply, k, bi)
    kk = _each(_bdot_nt, kb, k)
    amat = _each(lambda a, d: jnp.where(strict, a * d, 0.0), kk, decay)
    ad = _each(lambda a: jnp.where(same_block, a, 0.0), amat)
    an = _each(jnp.subtract, amat, ad)
    doff = _each(jnp.negative, ad)
    pw = ad
    for _ in range(DN_SOLVE_BLOCK.bit_length() - 2):
        pw = _each(_bdot, pw, pw)
        dp = _each(_bdot, doff, pw)
        doff = _each(lambda d, p, x: d + p + x, doff, pw, dp)
    assert n // DN_SOLVE_BLOCK == 4
    bm = _each(jnp.add, an, _each(_bdot, doff, an))
    rhs = _each(lambda vv, b, kk_, e: jnp.concatenate([vv * b, kk_ * e], -1), v, bi, kb, eg)
    sol = _each(jnp.add, rhs, _each(_bdot, doff, rhs))
    b2 = _each(_bdot, bm, bm)
    sol = _each(jnp.add, sol, _each(_bdot, b2, sol))
    sol = _each(jnp.subtract, sol, _each(_bdot, bm, sol))
    qk = _each(jnp.multiply, _each(_bdot_nt, q, k), decay)
    g_last = _each(lambda a: a[n - 1:n, :], gi)
    return (_each(lambda x: x[:, :DN_DK], sol), _each(lambda x: x[:, DN_DK:], sol), qk, _each(jnp.multiply, q, eg),
            _each(lambda kk_, gl, a: kk_ * jnp.exp(gl - a), k, g_last, gi), _each(jnp.exp, g_last))


def _delta_prompt_kernel(q_ref, k_ref, v_ref, zg_ref, ab_ref, alog_ref, dtb_ref, ng_ref, o_ref, sfin_ref, s_ref):
    c = pl.program_id(1)
    n = DN_CHUNK
    heads = range(DN_HEADS)
    chunks = range(q_ref.shape[0] // n)

    @pl.when(c == 0)
    def _():
        s_ref[...] = jnp.zeros_like(s_ref)

    row = lax.broadcasted_iota(jnp.int32, (n, n), 0)
    col = lax.broadcasted_iota(jnp.int32, (n, n), 1)
    lower = row >= col
    masks = (lower, row > col, (row // DN_SOLVE_BLOCK) == (col // DN_SOLVE_BLOCK))
    tri = lower.astype(F32)
    rows = [slice(j * n, (j + 1) * n) for j in chunks]
    cols = [slice(h * DN_DK, (h + 1) * DN_DK) for h in heads]
    gate = [_gate_terms(ab_ref[r, :], alog_ref[...], dtb_ref[...]) for r in rows]
    gcol = [_dot_hi(tri, g) for g, _ in gate]
    grow = [_dot_nt_hi(g.T, tri) for g, _ in gate]
    units = [(j, h) for j in chunks for h in heads]
    u, w, qk, qd, kd, gl = _delta_chunk_terms(
        [q_ref[rows[j], cols[h]] for j, h in units], [k_ref[rows[j], cols[h]] for j, h in units],
        [v_ref[rows[j], cols[h]] for j, h in units], [gcol[j][:, h:h + 1] for j, h in units],
        [grow[j][h:h + 1, :] for j, h in units],
        [gate[j][1][:, DN_HEADS + h:DN_HEADS + h + 1] for j, h in units], masks)
    s = [s_ref[h] for h in heads]
    for j in chunks:
        at = lambda xs: xs[j * DN_HEADS:(j + 1) * DN_HEADS]
        v_new = _each(jnp.subtract, at(u), _each(_bdot, at(w), s))
        o = _each(jnp.add, _each(_bdot, at(qd), s), _each(_bdot, at(qk), v_new))
        s = _each(lambda ss, g, x: ss * g + x, s, at(gl), _each(_bdot_tn, at(kd), v_new))
        for h in heads:
            on = o[h] * lax.rsqrt(jnp.mean(o[h] * o[h], -1, keepdims=True) + EPS) * ng_ref[...]
            o_ref[rows[j], cols[h]] = (on * _silu(zg_ref[rows[j], cols[h]])).astype(BF16)
    for h in heads:
        s_ref[h] = s[h]

    @pl.when(c == pl.num_programs(1) - 1)
    def _():
        sfin_ref[...] = s_ref[...]


def _delta_prompt(qkv3, z3, zg_col, ab3, lyr, n_lyr, a_log, dt_bias, norm_g, carried):
    b, t_len, _ = qkv3.shape
    ch = DN_HEADS * DN_DK
    n = DN_CHUNK * DN_STEP_CHUNKS
    stacked = {1: ((None, DN_HEADS, DN_DK, DN_DK), lambda i, c: (i, 0, 0, 0))}
    call, extra = _stacked_call(
        _delta_prompt_kernel, 8, carried, stacked, lyr,
        grid=(b, t_len // n),
        in_specs=[
            pl.BlockSpec((None, n, ch), lambda i, c: (i, c, 0)),
            pl.BlockSpec((None, n, ch), lambda i, c: (i, c, 1)),
            pl.BlockSpec((None, n, ch), lambda i, c: (i, c, 2)),
            pl.BlockSpec((None, n, ch), lambda i, c: (i, c, zg_col)),
            pl.BlockSpec((None, n, LANES), lambda i, c: (i, c, 0)),
            _whole(lyr, a_log), _whole(lyr, dt_bias), _whole(lyr, norm_g),
        ],
        out_specs=[
            pl.BlockSpec((None, n, ch), lambda i, c: (i, c, 0)),
            None,
        ],
        out_shape=[jax.ShapeDtypeStruct((b, t_len, ch), BF16),
                   jax.ShapeDtypeStruct((n_lyr, b, DN_HEADS, DN_DK, DN_DK), F32)],
        scratch_shapes=[pltpu.VMEM((DN_HEADS, DN_DK, DN_DK), F32)],
        compiler_params=_params("parallel", "arbitrary"),
        name="delta_prompt",
    )
    return call(qkv3, qkv3, qkv3, z3, ab3, a_log, dt_bias, norm_g, *extra)


def _odd_sample_kernel(p_ref, gt_ref, q_ref, k_ref, v_ref, zg_ref, ab_ref, cst_ref, dst_ref, s_ref,
                       dw_ref, db_ref, lg_ref, lb_ref, cw_ref, alog_ref, dtb_ref, ng_ref,
                       oc_ref, od_ref, ncst_ref, ndst_ref, ns_ref, obuf_ref):
    bt, ch = p_ref.shape
    hist = CONV_WIDTH - 1
    glu = p_ref[...] * jax.nn.sigmoid(gt_ref[...])
    acc = db_ref[...] + dw_ref[hist:hist + 1, :] * glu
    for k in range(hist):
        acc = acc + dw_ref[k:k + 1, :] * cst_ref[k]
    oc_ref[...] = _silu(_layernorm_rows(acc, lg_ref[...], lb_ref[...])).astype(BF16)
    ncst_ref[0:hist - 1] = cst_ref[1:hist]
    ncst_ref[hist - 1] = glu

    dh = DN_CONV - 1
    parts = []
    for pi, x_ref in enumerate((q_ref, k_ref, v_ref)):
        cols = slice(pi * ch, (pi + 1) * ch)
        x = x_ref[...]
        y = cw_ref[dh:dh + 1, cols] * x
        for k in range(dh):
            y = y + cw_ref[k:k + 1, cols] * dst_ref[k, :, cols]
        parts.append(_silu(y))
        ndst_ref[0:dh - 1, :, cols] = dst_ref[1:dh, :, cols]
        ndst_ref[dh - 1, :, cols] = x
    q = _l2norm_heads(parts[0], DN_DK ** -0.5)
    k = _l2norm_heads(parts[1], 1.0)
    v = parts[2]
    g, beta = _gate_terms(ab_ref[...], alog_ref[...], dtb_ref[...])
    eg_all = jnp.exp(g)

    assert 2 * DN_HEADS * bt == LANES
    kq = jnp.concatenate([x[:, h * DN_DK:(h + 1) * DN_DK] for x in (k, q) for h in range(DN_HEADS)], 0)
    kq_t = kq.T
    for h in range(DN_HEADS):
        cols = slice(h * DN_DK, (h + 1) * DN_DK)
        for b in range(bt):
            kc = kq_t[:, h * bt + b:h * bt + b + 1]
            qc = kq_t[:, (DN_HEADS + h) * bt + b:(DN_HEADS + h) * bt + b + 1]
            s = s_ref[b, h]
            sk = jnp.sum(s * kc, 0, keepdims=True)
            sq = jnp.sum(s * qc, 0, keepdims=True)
            eg = eg_all[b:b + 1, h:h + 1]
            bi = beta[b:b + 1, DN_HEADS + h:DN_HEADS + h + 1]
            v_new = bi * (v[b:b + 1, cols] - eg * sk)
            qk = jnp.sum(q[b:b + 1, cols] * k[b:b + 1, cols], -1, keepdims=True)
            obuf_ref[b:b + 1, cols] = eg * sq + qk * v_new
            ns_ref[b, h] = s * eg + kc * v_new
    for h in range(DN_HEADS):
        cols = slice(h * DN_DK, (h + 1) * DN_DK)
        o = obuf_ref[:, cols]
        o = o * lax.rsqrt(jnp.mean(o * o, -1, keepdims=True) + EPS) * ng_ref[...]
        od_ref[:, cols] = (o * _silu(zg_ref[:, cols])).astype(BF16)


def _odd_sample(z, ab, lyr, cstate, dstate, sstate, dw, db, ln_g, ln_b, conv_w, a_log, dt_bias, norm_g, carried, bt):
    n = z.shape[0]
    ch = db.shape[-1]
    zcol = lambda c: pl.BlockSpec((bt, ch), lambda i: (i, c))
    lead_blk = lambda a: ((bt,) + a.shape[2:], lambda i: (i,) + (0,) * (a.ndim - 2))
    hist_blk = lambda a: ((a.shape[1], bt, a.shape[3]), lambda i: (0, i, 0))
    lead = lambda a: _layer(lyr, *lead_blk(a))
    hist = lambda a: _layer(lyr, *hist_blk(a))
    act = jax.ShapeDtypeStruct((n, ch), BF16)
    same = lambda a: jax.ShapeDtypeStruct(a.shape, F32)
    stacked = {2: hist_blk(cstate), 3: hist_blk(dstate), 4: lead_blk(sstate)}
    call, extra = _stacked_call(
        _odd_sample_kernel, 18, carried, stacked, lyr,
        grid=(n // bt,),
        in_specs=[zcol(0), zcol(1), zcol(2), zcol(3), zcol(4), zcol(5),
                  pl.BlockSpec((bt, LANES), lambda i: (i, 0)),
                  hist(cstate), hist(dstate), lead(sstate),
                  _whole(lyr, dw), _whole(lyr, db), _whole(lyr, ln_g), _whole(lyr, ln_b),
                  _whole(lyr, conv_w), _whole(lyr, a_log), _whole(lyr, dt_bias), _whole(lyr, norm_g)],
        out_specs=[zcol(0), zcol(0), None, None, None],
        out_shape=[act, act, same(cstate), same(dstate), same(sstate)],
        scratch_shapes=[pltpu.VMEM((bt, ch), F32)],
        compiler_params=_params("parallel"),
        name="odd_sample",
    )
    return call(z, z, z, z, z, z, ab, cstate, dstate, sstate, dw, db, ln_g, ln_b, conv_w, a_log, dt_bias, norm_g,
                *extra)


def kernel(x_prompt, x_sample, state_pool, state_conv_c, state_dn_conv, state_dn_S, norm_mix, norm_ffn, norm_final, ev_w_in, pool_w, pool_scale, sgu_ln_g, sgu_ln_b, sgu_ws, sgu_b, ev_w_out, od_w_in, cv_dw, cv_db, cv_ln_g, cv_ln_b, dn_conv_w, dn_a_log, dn_dt_bias, dn_norm_g, od_w_out, ffn_w_up, ffn_w_down):
    bp, t_len, d = x_prompt.shape
    bs = x_sample.shape[0]
    assert x_sample.shape[1] == 1
    depth = norm_mix.shape[0]
    n_even, n_odd = ev_w_in.shape[0], od_w_in.shape[0]
    ch = pool_scale.shape[1]
    odd_main = 6 * ch
    assert od_w_in.shape[2] == odd_main + 2 * DN_HEADS

    rows = lambda a: a.reshape(a.shape[0], 1, a.shape[-1])
    lane_rows = lambda a: rows(jnp.pad(a, ((0, 0), (0, LANES - a.shape[1]))))
    ev_in = ev_w_in.astype(BF16)
    ev_out = ev_w_out.astype(BF16)
    od_in = od_w_in[:, :, :odd_main].astype(BF16)
    od_ab = jnp.pad(od_w_in[:, :, odd_main:], ((0, 0), (0, 0), (0, LANES - 2 * DN_HEADS))).astype(BF16)
    od_out = od_w_out.astype(BF16)
    w_up = ffn_w_up.astype(BF16)
    w_down = ffn_w_down.astype(BF16)
    pool_wb = pool_w.astype(BF16)
    g_mix, g_ffn, g_fin = rows(norm_mix), rows(norm_ffn), norm_final.reshape(1, d)
    p_scale, s_lg, s_lb = rows(pool_scale), rows(sgu_ln_g), rows(sgu_ln_b)
    s_bt = jnp.swapaxes(sgu_b, 1, 2)
    s_w0 = rows(jnp.repeat(sgu_ws[:, :, 0, 0], ch // SGU_HEADS, axis=1))
    s_b0 = rows(jnp.repeat(sgu_b[:, :, 0], ch // SGU_HEADS, axis=1))
    c_db, c_lg, c_lb = rows(cv_db), rows(cv_ln_g), rows(cv_ln_b)
    a_log, dt_bias, d_ng = lane_rows(dn_a_log), lane_rows(dn_dt_bias), rows(dn_norm_g)

    hist_major = lambda a: jnp.swapaxes(a, 1, 2)
    st_pool, st_cc, st_dc = hist_major(state_pool), hist_major(state_conv_c), hist_major(state_dn_conv)

    xp = x_prompt.reshape(bp * t_len, d)
    xs = x_sample.reshape(bs, d)
    tn = _tiles(bp * t_len)[1]
    flat = lambda a: a.reshape(bp * t_len, a.shape[-1])
    pool_p, dc_p = [], []
    even_s = cc_p = s_p = odd_s = None
    for l in range(depth):
        i = l // 2
        if l % 2 == 0:
            zp = _norm_matmul(xp, g_mix, l, ev_in, i, 3 * ch, tn)
            zs = _norm_matmul(xs, g_mix, l, ev_in, i, 3 * ch, tn)
            zp3 = zp.reshape(bp, t_len, 3 * ch)
            a_p, b_p = _even_prompt(zp3, i, pool_wb, p_scale, s_lg, s_lb, sgu_ws, s_bt, 256)
            a_s, b_s, *even_s = _even_sample(zs, i, st_pool, pool_wb, p_scale, s_lg, s_lb, s_w0, s_b0, even_s, 8)
            pool_p.append(zp3[:, t_len - POOL_HIST:, :ch])
            xp = _mix_out(flat(a_p), flat(b_p), ev_out, i, xp)
            xs = _mix_out(a_s, b_s, ev_out, i, xs)
        else:
            zp = _norm_matmul(xp, g_mix, l, od_in, i, odd_main, tn)
            zs = _norm_matmul(xs, g_mix, l, od_in, i, odd_main, tn)
            abp = _norm_matmul(xp, g_mix, l, od_ab, i, LANES, LANES)
            abs_ = _norm_matmul(xs, g_mix, l, od_ab, i, LANES, LANES)
            zp3 = zp.reshape(bp, t_len, odd_main)
            c_p, *cc_p = _conv_prompt(zp3, i, n_odd, cv_dw, c_db, c_lg, c_lb, cc_p, 256)
            qkv3 = _dn_prep(zp3, i, dn_conv_w, 2, 256)
            d_p, *s_p = _delta_prompt(qkv3, zp3, 5, abp.reshape(bp, t_len, LANES), i, n_odd, a_log, dt_bias, d_ng,
                                      s_p)
            c_s, d_s, *odd_s = _odd_sample(zs, abs_, i, st_cc, st_dc, state_dn_S, cv_dw, c_db, c_lg,
                                           c_lb, dn_conv_w, a_log, dt_bias, d_ng, odd_s, 8)
            dc_p.append(zp3[:, t_len - (DN_CONV - 1):, 2 * ch:5 * ch])
            xp = _mix_out(flat(c_p), flat(d_p), od_out, i, xp)
            xs = _mix_out(c_s, d_s, od_out, i, xs)
        last = l == depth - 1
        xp = _ffn(xp, g_ffn, l, w_up, w_down, g_fin, last)
        xs = _ffn(xs, g_ffn, l, w_up, w_down, g_fin, last)
    pool_s, v_s = even_s
    cc_s, dc_s, s_s = odd_s
    return (xp.reshape(bp, t_len, d), xs.reshape(bs, 1, d), jnp.stack(pool_p), hist_major(pool_s),
            v_s.reshape(n_even, bs, 1, ch), cc_p[0], hist_major(cc_s), jnp.stack(dc_p), hist_major(dc_s), s_p[0], s_s)
```

```python
import functools

import jax
import jax.numpy as jnp
from jax import lax
from jax.experimental import pallas as pl
from jax.experimental.pallas import tpu as pltpu

F32 = jnp.float32
BF16 = jnp.bfloat16
HI = lax.Precision.HIGHEST

EPS = 1e-6
PAST_LEN = 16384
POOL_WINDOWS = (2, 4, 8, 16)
POOL_HIST = max(POOL_WINDOWS) - 1
SGU_HEADS = 4
SGU_CHUNK = 128
CONV_WIDTH = 31
DN_HEADS = 8
DN_DK = 128
DN_CONV = 4
DN_CHUNK = 64
DN_SOLVE_BLOCK = 16
DN_STEP_CHUNKS = 2

LANES = 128
VMEM_LIMIT = 48 * 1024 * 1024

POOL_HALO = 16
CONV_HALO = 32
DN_HALO = 8


def _tiles(m):
    return min(m, 1024), min(m, 512), 512


def _params(*semantics):
    return pltpu.CompilerParams(dimension_semantics=semantics, vmem_limit_bytes=VMEM_LIMIT)


def _layer(lyr, block, index_map):
    return pl.BlockSpec((None,) + tuple(block), lambda *g: (lyr,) + tuple(index_map(*g)))


def _whole(lyr, arr):
    return pl.BlockSpec((None,) + arr.shape[1:], lambda *g: (lyr,) + (0,) * (arr.ndim - 1))


def _skip_refs(fn, start, count):
    def body(*refs):
        return fn(*refs[:start], *refs[start + count:])
    return body


def _own_layer_zero_rest(fn, n_in, outs, lyr):
    def body(*refs):
        refs = list(refs)
        for o in outs:
            full = refs[n_in + o]
            for other in range(full.shape[0]):
                if other != lyr:
                    full[other] = jnp.zeros(full.shape[1:], full.dtype)
            refs[n_in + o] = full.at[lyr]
        return fn(*refs)
    return body


def _stacked_call(kernel_fn, n_in, carried, stacked_out, lyr, **kw):
    out_specs = list(kw.pop("out_specs"))
    if carried is None:
        for o, (block, imap) in stacked_out.items():
            n_lyr = kw["out_shape"][o].shape[0]
            out_specs[o] = pl.BlockSpec((n_lyr,) + tuple(block), lambda *g, imap=imap: (0,) + tuple(imap(*g)))
        body = _own_layer_zero_rest(kernel_fn, n_in, tuple(stacked_out), lyr)
        return pl.pallas_call(body, out_specs=out_specs, **kw), ()
    for o, (block, imap) in stacked_out.items():
        out_specs[o] = _layer(lyr, block, imap)
    kw["in_specs"] = list(kw["in_specs"]) + [pl.BlockSpec(memory_space=pl.ANY)] * len(carried)
    aliases = {n_in + k: o for k, o in enumerate(stacked_out)}
    body = _skip_refs(kernel_fn, n_in, len(carried))
    return pl.pallas_call(body, out_specs=out_specs, input_output_aliases=aliases, **kw), tuple(carried)


def _rms_rows(x, g):
    return x * lax.rsqrt(jnp.mean(x * x, -1, keepdims=True) + EPS) * g


def _layernorm_rows(x, g, b):
    mu = jnp.mean(x, -1, keepdims=True)
    xc = x - mu
    var = jnp.mean(xc * xc, -1, keepdims=True)
    return xc * lax.rsqrt(var + EPS) * g + b


def _gelu(x):
    return 0.5 * x * (1.0 + lax.erf(x * 0.7071067811865476))


def _silu(x):
    return x * jax.nn.sigmoid(x)


def _dot(a, b):
    return jnp.dot(a, b, preferred_element_type=F32)


def _dot_hi(a, b):
    return jnp.dot(a, b, precision=HI, preferred_element_type=F32)


def _dot_nt_hi(a, b):
    return lax.dot_general(a, b, (((1,), (1,)), ((), ())), precision=HI, preferred_element_type=F32)


def _bdot(a, b):
    return jnp.dot(a.astype(BF16), b.astype(BF16), preferred_element_type=F32)


def _bdot_nt(a, b):
    return lax.dot_general(a.astype(BF16), b.astype(BF16), (((1,), (1,)), ((), ())), preferred_element_type=F32)


def _bdot_tn(a, b):
    return lax.dot_general(a.astype(BF16), b.astype(BF16), (((0,), (0,)), ((), ())), preferred_element_type=F32)


def _norm_matmul_kernel(x_ref, g_ref, w_ref, o_ref, h_ref, *, w_rows_out):
    @pl.when(pl.program_id(1) == 0)
    def _():
        h_ref[...] = _rms_rows(x_ref[...], g_ref[...]).astype(BF16)

    w = w_ref[...].astype(BF16)
    if w_rows_out:
        o_ref[...] = lax.dot_general(h_ref[...], w, (((1,), (1,)), ((), ())), preferred_element_type=F32)
    else:
        o_ref[...] = _dot(h_ref[...], w)


def _norm_matmul(x, gains, lyr, w, wl, n_out, tn, w_rows_out=False, col0=0):
    m, d = x.shape
    tm = _tiles(m)[0]
    if w_rows_out:
        w_spec = _layer(wl, (tn, d), lambda i, j: (col0 + j, 0))
    else:
        w_spec = _layer(wl, (d, tn), lambda i, j: (0, col0 + j))
    return pl.pallas_call(
        functools.partial(_norm_matmul_kernel, w_rows_out=w_rows_out),
        grid=(m // tm, n_out // tn),
        in_specs=[
            pl.BlockSpec((tm, d), lambda i, j: (i, 0)),
            _whole(lyr, gains),
            w_spec,
        ],
        out_specs=pl.BlockSpec((tm, tn), lambda i, j: (i, j)),
        out_shape=jax.ShapeDtypeStruct((m, n_out), F32),
        scratch_shapes=[pltpu.VMEM((tm, d), BF16)],
        compiler_params=_params("parallel", "arbitrary"),
        name="norm_matmul",
    )(x, gains, w)


def _mix_out_kernel(a0_ref, a1_ref, w0_ref, w1_ref, x_ref, o_ref):
    o_ref[...] = x_ref[...] + _dot(a0_ref[...], w0_ref[...]) + _dot(a1_ref[...], w1_ref[...])


def _mix_out(a0, a1, w, wl, x):
    m, kh = a0.shape
    n = w.shape[2]
    tm = _tiles(m)[1]
    w_half = lambda r: pl.BlockSpec((None, kh, n), lambda i: (wl, r, 0), pipeline_mode=pl.Buffered(1))
    return pl.pallas_call(
        _mix_out_kernel,
        grid=(m // tm,),
        in_specs=[
            pl.BlockSpec((tm, kh), lambda i: (i, 0)),
            pl.BlockSpec((tm, kh), lambda i: (i, 0)),
            w_half(0), w_half(1),
            pl.BlockSpec((tm, n), lambda i: (i, 0)),
        ],
        out_specs=pl.BlockSpec((tm, n), lambda i: (i, 0)),
        out_shape=jax.ShapeDtypeStruct((m, n), F32),
        compiler_params=_params("parallel"),
        name="mix_out",
    )(a0, a1, w, w, x)


def _ffn_kernel(x_ref, g_ref, wg_ref, wu_ref, wd_ref, gf_ref, o_ref, h_ref, *, final_norm):
    f = pl.program_id(1)

    @pl.when(f == 0)
    def _():
        x = x_ref[...]
        h_ref[...] = _rms_rows(x, g_ref[...]).astype(BF16)
        o_ref[...] = x

    h = h_ref[...]
    act = _silu(_dot(h, wg_ref[...])) * _dot(h, wu_ref[...])
    o_ref[...] += _dot(act.astype(BF16), wd_ref[...])

    if final_norm:
        @pl.when(f == pl.num_programs(1) - 1)
        def _():
            o_ref[...] = _rms_rows(o_ref[...], gf_ref[...])


def _ffn(x, gains, lyr, w_up, w_down, g_final, final_norm):
    m, d = x.shape
    d_ff = w_down.shape[1]
    _, tm, tf = _tiles(m)
    nf = d_ff // tf
    return pl.pallas_call(
        functools.partial(_ffn_kernel, final_norm=final_norm),
        grid=(m // tm, nf),
        in_specs=[
            pl.BlockSpec((tm, d), lambda i, f: (i, 0)),
            _whole(lyr, gains),
            _layer(lyr, (d, tf), lambda i, f: (0, f)),
            _layer(lyr, (d, tf), lambda i, f: (0, f + nf)),
            _layer(lyr, (tf, d), lambda i, f: (f, 0)),
            pl.BlockSpec((1, d), lambda i, f: (0, 0)),
        ],
        out_specs=pl.BlockSpec((tm, d), lambda i, f: (i, 0)),
        out_shape=jax.ShapeDtypeStruct((m, d), F32),
        scratch_shapes=[pltpu.VMEM((tm, d), BF16)],
        compiler_params=_params("parallel", "arbitrary"),
        name="ffn",
    )(x, gains, w_up, w_up, w_down, g_final)


def _even_prompt_kernel(a_ref, halo_ref, u_ref, v_ref, pw_ref, ps_ref, lg_ref, lb_ref, ws_ref, bst_ref,
                        oa_ref, ob_ref, buf_ref):
    t = pl.program_id(1)
    tt, ch = a_ref.shape
    gch = ch // len(POOL_WINDOWS)
    a = a_ref[...]
    buf_ref[0:POOL_HALO, :] = jnp.where(t > 0, halo_ref[...], 0.0)
    buf_ref[POOL_HALO:, :] = a
    pos = t * tt + lax.broadcasted_iota(jnp.int32, (tt, 1), 0)
    for gi, w in enumerate(POOL_WINDOWS):
        cols = slice(gi * gch, (gi + 1) * gch)
        s = a[:, cols]
        for k in range(1, w):
            s = s + buf_ref[POOL_HALO - k:POOL_HALO - k + tt, cols]
        cnt = jnp.minimum(w, pos + 1).astype(F32)
        pooled = s / cnt - a[:, cols]
        mixed = _dot(pooled.astype(BF16), pw_ref[gi])
        oa_ref[:, cols] = (mixed * ps_ref[:, cols]).astype(BF16)

    u = _gelu(u_ref[...])
    vn = _layernorm_rows(_gelu(v_ref[...]), lg_ref[...], lb_ref[...])
    hch = ch // SGU_HEADS
    row = lax.broadcasted_iota(jnp.int32, (SGU_CHUNK, SGU_CHUNK), 0)
    col = lax.broadcasted_iota(jnp.int32, (SGU_CHUNK, SGU_CHUNK), 1)
    for h in range(SGU_HEADS):
        cols = slice(h * hch, (h + 1) * hch)
        wm = jnp.where(row >= col, ws_ref[h], 0.0).astype(BF16)
        bias = bst_ref[:, h:h + 1]
        for c in range(tt // SGU_CHUNK):
            rows = slice(c * SGU_CHUNK, (c + 1) * SGU_CHUNK)
            mixed = _dot(wm, vn[rows, cols].astype(BF16)) + bias
            ob_ref[rows, cols] = (u[rows, cols] * mixed).astype(BF16)


def _even_prompt(z3, lyr, pool_w, pool_scale, ln_g, ln_b, ws, bst, tt):
    b, t_len, _ = z3.shape
    ch = pool_scale.shape[-1]
    hb = tt // POOL_HALO
    out = jax.ShapeDtypeStruct((b, t_len, ch), BF16)
    return pl.pallas_call(
        _even_prompt_kernel,
        grid=(b, t_len // tt),
        in_specs=[
            pl.BlockSpec((None, tt, ch), lambda i, t: (i, t, 0)),
            pl.BlockSpec((None, POOL_HALO, ch), lambda i, t: (i, jnp.maximum(t * hb - 1, 0), 0)),
            pl.BlockSpec((None, tt, ch), lambda i, t: (i, t, 1)),
            pl.BlockSpec((None, tt, ch), lambda i, t: (i, t, 2)),
            _whole(lyr, pool_w), _whole(lyr, pool_scale), _whole(lyr, ln_g), _whole(lyr, ln_b),
            _whole(lyr, ws), _whole(lyr, bst),
        ],
        out_specs=[pl.BlockSpec((None, tt, ch), lambda i, t: (i, t, 0))] * 2,
        out_shape=[out, out],
        scratch_shapes=[pltpu.VMEM((POOL_HALO + tt, ch), F32)],
        compiler_params=_params("parallel", "arbitrary"),
        name="even_prompt",
    )(z3, z3, z3, z3, pool_w, pool_scale, ln_g, ln_b, ws, bst)


def _even_sample_kernel(a_ref, u_ref, v_ref, st_ref, pw_ref, ps_ref, lg_ref, lb_ref, w0_ref, b0_ref,
                        oa_ref, ob_ref, nst_ref, nv_ref):
    ch = a_ref.shape[1]
    gch = ch // len(POOL_WINDOWS)
    a = a_ref[...]
    for gi, w in enumerate(POOL_WINDOWS):
        cols = slice(gi * gch, (gi + 1) * gch)
        s = a[:, cols]
        for k in range(1, w):
            s = s + st_ref[POOL_HIST - k, :, cols]
        pooled = s / float(min(w, PAST_LEN + 1)) - a[:, cols]
        mixed = _dot(pooled.astype(BF16), pw_ref[gi])
        oa_ref[:, cols] = (mixed * ps_ref[:, cols]).astype(BF16)
    nst_ref[0:POOL_HIST - 1] = st_ref[1:POOL_HIST]
    nst_ref[POOL_HIST - 1] = a

    u = _gelu(u_ref[...])
    vn = _layernorm_rows(_gelu(v_ref[...]), lg_ref[...], lb_ref[...])
    nv_ref[...] = vn
    ob_ref[...] = (u * (vn * w0_ref[...] + b0_ref[...])).astype(BF16)


def _even_sample(z, lyr, state, pool_w, pool_scale, ln_g, ln_b, w0, b0, carried, bt):
    assert PAST_LEN % SGU_CHUNK == 0
    n = z.shape[0]
    ch = pool_scale.shape[-1]
    act = jax.ShapeDtypeStruct((n, ch), BF16)
    rows = pl.BlockSpec((bt, ch), lambda i: (i, 0))
    n_in = 10
    stacked = {2: ((POOL_HIST, bt, ch), lambda i: (0, i, 0)), 3: ((bt, ch), lambda i: (i, 0))}
    call, extra = _stacked_call(
        _even_sample_kernel, n_in, carried, stacked, lyr,
        grid=(n // bt,),
        in_specs=[
            rows, pl.BlockSpec((bt, ch), lambda i: (i, 1)), pl.BlockSpec((bt, ch), lambda i: (i, 2)),
            _layer(lyr, (POOL_HIST, bt, ch), lambda i: (0, i, 0)),
            _whole(lyr, pool_w), _whole(lyr, pool_scale), _whole(lyr, ln_g), _whole(lyr, ln_b),
            _whole(lyr, w0), _whole(lyr, b0),
        ],
        out_specs=[rows, rows, None, None],
        out_shape=[act, act, jax.ShapeDtypeStruct(state.shape, F32),
                   jax.ShapeDtypeStruct((state.shape[0], n, ch), F32)],
        compiler_params=_params("parallel"),
        name="even_sample",
    )
    return call(z, z, z, state, pool_w, pool_scale, ln_g, ln_b, w0, b0, *extra)


def _conv_prompt_kernel(p_ref, q_ref, hp_ref, hq_ref, dw_ref, db_ref, lg_ref, lb_ref, o_ref, st_ref, buf_ref):
    t = pl.program_id(1)
    tt = p_ref.shape[0]
    hist = CONV_WIDTH - 1
    glu = p_ref[...] * jax.nn.sigmoid(q_ref[...])
    halo = hp_ref[...] * jax.nn.sigmoid(hq_ref[...])
    buf_ref[0:CONV_HALO, :] = jnp.where(t > 0, halo, 0.0)
    buf_ref[CONV_HALO:, :] = glu
    base = CONV_HALO - hist
    acc = db_ref[...] + dw_ref[hist:hist + 1, :] * glu
    for k in range(hist):
        acc = acc + dw_ref[k:k + 1, :] * buf_ref[base + k:base + k + tt, :]
    o_ref[...] = _silu(_layernorm_rows(acc, lg_ref[...], lb_ref[...])).astype(BF16)

    @pl.when(t == pl.num_programs(1) - 1)
    def _():
        st_ref[...] = buf_ref[CONV_HALO + tt - hist:CONV_HALO + tt, :]


def _conv_prompt(z3, lyr, n_lyr, dw, db, ln_g, ln_b, carried, tt):
    b, t_len, _ = z3.shape
    ch = db.shape[-1]
    hb = tt // CONV_HALO
    hist = CONV_WIDTH - 1
    halo = lambda c: pl.BlockSpec((None, CONV_HALO, ch), lambda i, t: (i, jnp.maximum(t * hb - 1, 0), c))
    stacked = {1: ((None, hist, ch), lambda i, t: (i, 0, 0))}
    call, extra = _stacked_call(
        _conv_prompt_kernel, 8, carried, stacked, lyr,
        grid=(b, t_len // tt),
        in_specs=[
            pl.BlockSpec((None, tt, ch), lambda i, t: (i, t, 0)),
            pl.BlockSpec((None, tt, ch), lambda i, t: (i, t, 1)),
            halo(0), halo(1),
            _whole(lyr, dw), _whole(lyr, db), _whole(lyr, ln_g), _whole(lyr, ln_b),
        ],
        out_specs=[
            pl.BlockSpec((None, tt, ch), lambda i, t: (i, t, 0)),
            None,
        ],
        out_shape=[jax.ShapeDtypeStruct((b, t_len, ch), BF16), jax.ShapeDtypeStruct((n_lyr, b, hist, ch), F32)],
        scratch_shapes=[pltpu.VMEM((CONV_HALO + tt, ch), F32)],
        compiler_params=_params("parallel", "arbitrary"),
        name="conv_prompt",
    )
    return call(z3, z3, z3, z3, dw, db, ln_g, ln_b, *extra)


def _l2norm_heads(x, scale):
    outs = []
    for h in range(x.shape[1] // DN_DK):
        xh = x[:, h * DN_DK:(h + 1) * DN_DK]
        outs.append(xh * (lax.rsqrt(jnp.sum(xh * xh, -1, keepdims=True) + EPS) * scale))
    return jnp.concatenate(outs, -1)


def _dn_short_conv(raw_refs, w_ref, first, tail_ref, buf_ref, out_ref):
    hist = DN_CONV - 1
    base = DN_HALO - hist
    for part, (x_ref, scale) in enumerate(zip(raw_refs, (DN_DK ** -0.5, 1.0, None))):
        tt, ch = x_ref.shape
        cols = slice(part * ch, (part + 1) * ch)
        x = x_ref[...]
        buf_ref[0:DN_HALO, :] = jnp.where(first, 0.0, tail_ref[part])
        buf_ref[DN_HALO:, :] = x
        tail_ref[part] = x[tt - DN_HALO:, :]
        acc = w_ref[hist:hist + 1, cols] * x
        for k in range(hist):
            acc = acc + w_ref[k:k + 1, cols] * buf_ref[base + k:base + k + tt, :]
        y = _silu(acc)
        out_ref[part] = y if scale is None else _l2norm_heads(y, scale)


def _gate_terms(ab, alog, dtb):
    g = -jnp.exp(alog) * jax.nn.softplus(ab + dtb)
    return g, jax.nn.sigmoid(ab)


def _each(fn, *lists):
    return [fn(*xs) for xs in zip(*lists)]


def _delta_chunk_terms(q, k, v, gi, gj, bi, masks):
    lower, strict, same_block = masks
    n = q[0].shape[0]
    decay = _each(lambda a, b: jnp.where(lower, jnp.exp(jnp.where(lower, a - b, 0.0)), 0.0), gi, gj)
    eg = _each(jnp.exp, gi)
    kb = _each(jnp.multiply, k, bi)
    kk = _each(_bdot_nt, kb, k)
    amat = _each(lambda a, d: jnp.where(strict, a * d, 0.0), kk, decay)
    ad = _each(lambda a: jnp.where(same_block, a, 0.0), amat)
    an = _each(jnp.subtract, amat, ad)
    doff = _each(jnp.negative, ad)
    pw = ad
    for _ in range(DN_SOLVE_BLOCK.bit_length() - 2):
        pw = _each(_bdot, pw, pw)
        dp = _each(_bdot, doff, pw)
        doff = _each(lambda d, p, x: d + p + x, doff, pw, dp)
    assert n // DN_SOLVE_BLOCK == 4
    bm = _each(jnp.add, an, _each(_bdot, doff, an))
    rhs = _each(lambda vv, b, kk_, e: jnp.concatenate([vv * b, kk_ * e], -1), v, bi, kb, eg)
    sol = _each(jnp.add, rhs, _each(_bdot, doff, rhs))
    b2 = _each(_bdot, bm, bm)
    sol = _each(jnp.add, sol, _each(_bdot, b2, sol))
    sol = _each(jnp.subtract, sol, _each(_bdot, bm, sol))
    qk = _each(jnp.multiply, _each(_bdot_nt, q, k), decay)
    g_last = _each(lambda a: a[n - 1:n, :], gi)
    return (_each(lambda x: x[:, :DN_DK], sol), _each(lambda x: x[:, DN_DK:], sol), qk, _each(jnp.multiply, q, eg),
            _each(lambda kk_, gl, a: kk_ * jnp.exp(gl - a), k, g_last, gi), _each(jnp.exp, g_last))


def _delta_prompt_kernel(q_raw, k_raw, v_raw, zg_ref, ab_ref, cw_ref, alog_ref, dtb_ref, ng_ref, o_ref, sfin_ref,
                         s_ref, tail_ref, buf_ref, qkv_ref):
    c = pl.program_id(1)
    n = DN_CHUNK
    heads = range(DN_HEADS)
    chunks = range(q_raw.shape[0] // n)

    @pl.when(c == 0)
    def _():
        s_ref[...] = jnp.zeros_like(s_ref)

    _dn_short_conv((q_raw, k_raw, v_raw), cw_ref, c == 0, tail_ref, buf_ref, qkv_ref)
    q_ref, k_ref, v_ref = qkv_ref.at[0], qkv_ref.at[1], qkv_ref.at[2]

    row = lax.broadcasted_iota(jnp.int32, (n, n), 0)
    col = lax.broadcasted_iota(jnp.int32, (n, n), 1)
    lower = row >= col
    masks = (lower, row > col, (row // DN_SOLVE_BLOCK) == (col // DN_SOLVE_BLOCK))
    tri = lower.astype(F32)
    rows = [slice(j * n, (j + 1) * n) for j in chunks]
    cols = [slice(h * DN_DK, (h + 1) * DN_DK) for h in heads]
    gate = [_gate_terms(ab_ref[r, :], alog_ref[...], dtb_ref[...]) for r in rows]
    gcol = [_dot_hi(tri, g) for g, _ in gate]
    grow = [_dot_nt_hi(g.T, tri) for g, _ in gate]
    units = [(j, h) for j in chunks for h in heads]
    u, w, qk, qd, kd, gl = _delta_chunk_terms(
        [q_ref[rows[j], cols[h]] for j, h in units], [k_ref[rows[j], cols[h]] for j, h in units],
        [v_ref[rows[j], cols[h]] for j, h in units], [gcol[j][:, h:h + 1] for j, h in units],
        [grow[j][h:h + 1, :] for j, h in units],
        [gate[j][1][:, DN_HEADS + h:DN_HEADS + h + 1] for j, h in units], masks)
    s = [s_ref[h] for h in heads]
    for j in chunks:
        at = lambda xs: xs[j * DN_HEADS:(j + 1) * DN_HEADS]
        v_new = _each(jnp.subtract, at(u), _each(_bdot, at(w), s))
        o = _each(jnp.add, _each(_bdot, at(qd), s), _each(_bdot, at(qk), v_new))
        s = _each(lambda ss, g, x: ss * g + x, s, at(gl), _each(_bdot_tn, at(kd), v_new))
        for h in heads:
            on = o[h] * lax.rsqrt(jnp.mean(o[h] * o[h], -1, keepdims=True) + EPS) * ng_ref[...]
            o_ref[rows[j], cols[h]] = (on * _silu(zg_ref[rows[j], cols[h]])).astype(BF16)
    for h in heads:
        s_ref[h] = s[h]

    @pl.when(c == pl.num_programs(1) - 1)
    def _():
        sfin_ref[...] = s_ref[...]


def _delta_prompt(z3, qkv_col, zg_col, ab3, lyr, n_lyr, conv_w, a_log, dt_bias, norm_g, carried):
    b, t_len, _ = z3.shape
    ch = DN_HEADS * DN_DK
    n = DN_CHUNK * DN_STEP_CHUNKS
    zcol = lambda col: pl.BlockSpec((None, n, ch), lambda i, c: (i, c, col))
    stacked = {1: ((None, DN_HEADS, DN_DK, DN_DK), lambda i, c: (i, 0, 0, 0))}
    call, extra = _stacked_call(
        _delta_prompt_kernel, 9, carried, stacked, lyr,
        grid=(b, t_len // n),
        in_specs=[
            zcol(qkv_col), zcol(qkv_col + 1), zcol(qkv_col + 2), zcol(zg_col),
            pl.BlockSpec((None, n, ab3.shape[-1]), lambda i, c: (i, c, 0)),
            _whole(lyr, conv_w), _whole(lyr, a_log), _whole(lyr, dt_bias), _whole(lyr, norm_g),
        ],
        out_specs=[
            pl.BlockSpec((None, n, ch), lambda i, c: (i, c, 0)),
            None,
        ],
        out_shape=[jax.ShapeDtypeStruct((b, t_len, ch), BF16),
                   jax.ShapeDtypeStruct((n_lyr, b, DN_HEADS, DN_DK, DN_DK), F32)],
        scratch_shapes=[pltpu.VMEM((DN_HEADS, DN_DK, DN_DK), F32), pltpu.VMEM((3, DN_HALO, ch), F32),
                        pltpu.VMEM((DN_HALO + n, ch), F32), pltpu.VMEM((3, n, ch), F32)],
        compiler_params=_params("parallel", "arbitrary"),
        name="delta_prompt",
    )
    return call(z3, z3, z3, z3, ab3, conv_w, a_log, dt_bias, norm_g, *extra)


def _odd_sample_kernel(p_ref, gt_ref, q_ref, k_ref, v_ref, zg_ref, ab_ref, cst_ref, dst_ref, s_ref,
                       dw_ref, db_ref, lg_ref, lb_ref, cw_ref, alog_ref, dtb_ref, ng_ref,
                       oc_ref, od_ref, ncst_ref, ndst_ref, ns_ref, obuf_ref):
    bt, ch = p_ref.shape
    hist = CONV_WIDTH - 1
    glu = p_ref[...] * jax.nn.sigmoid(gt_ref[...])
    acc = db_ref[...] + dw_ref[hist:hist + 1, :] * glu
    for k in range(hist):
        acc = acc + dw_ref[k:k + 1, :] * cst_ref[k]
    oc_ref[...] = _silu(_layernorm_rows(acc, lg_ref[...], lb_ref[...])).astype(BF16)
    ncst_ref[0:hist - 1] = cst_ref[1:hist]
    ncst_ref[hist - 1] = glu

    dh = DN_CONV - 1
    parts = []
    for pi, x_ref in enumerate((q_ref, k_ref, v_ref)):
        cols = slice(pi * ch, (pi + 1) * ch)
        x = x_ref[...]
        y = cw_ref[dh:dh + 1, cols] * x
        for k in range(dh):
            y = y + cw_ref[k:k + 1, cols] * dst_ref[k, :, cols]
        parts.append(_silu(y))
        ndst_ref[0:dh - 1, :, cols] = dst_ref[1:dh, :, cols]
        ndst_ref[dh - 1, :, cols] = x
    q = _l2norm_heads(parts[0], DN_DK ** -0.5)
    k = _l2norm_heads(parts[1], 1.0)
    v = parts[2]
    g, beta = _gate_terms(ab_ref[...], alog_ref[...], dtb_ref[...])
    eg_all = jnp.exp(g)

    assert 2 * DN_HEADS * bt == LANES
    kq = jnp.concatenate([x[:, h * DN_DK:(h + 1) * DN_DK] for x in (k, q) for h in range(DN_HEADS)], 0)
    kq_t = kq.T
    for h in range(DN_HEADS):
        cols = slice(h * DN_DK, (h + 1) * DN_DK)
        for b in range(bt):
            kc = kq_t[:, h * bt + b:h * bt + b + 1]
            qc = kq_t[:, (DN_HEADS + h) * bt + b:(DN_HEADS + h) * bt + b + 1]
            s = s_ref[b, h]
            sk = jnp.sum(s * kc, 0, keepdims=True)
            sq = jnp.sum(s * qc, 0, keepdims=True)
            eg = eg_all[b:b + 1, h:h + 1]
            bi = beta[b:b + 1, DN_HEADS + h:DN_HEADS + h + 1]
            v_new = bi * (v[b:b + 1, cols] - eg * sk)
            qk = jnp.sum(q[b:b + 1, cols] * k[b:b + 1, cols], -1, keepdims=True)
            obuf_ref[b:b + 1, cols] = eg * sq + qk * v_new
            ns_ref[b, h] = s * eg + kc * v_new
    for h in range(DN_HEADS):
        cols = slice(h * DN_DK, (h + 1) * DN_DK)
        o = obuf_ref[:, cols]
        o = o * lax.rsqrt(jnp.mean(o * o, -1, keepdims=True) + EPS) * ng_ref[...]
        od_ref[:, cols] = (o * _silu(zg_ref[:, cols])).astype(BF16)


def _odd_sample(z, ab, lyr, cstate, dstate, sstate, dw, db, ln_g, ln_b, conv_w, a_log, dt_bias, norm_g, carried, bt):
    n = z.shape[0]
    ch = db.shape[-1]
    zcol = lambda c: pl.BlockSpec((bt, ch), lambda i: (i, c))
    lead_blk = lambda a: ((bt,) + a.shape[2:], lambda i: (i,) + (0,) * (a.ndim - 2))
    hist_blk = lambda a: ((a.shape[1], bt, a.shape[3]), lambda i: (0, i, 0))
    lead = lambda a: _layer(lyr, *lead_blk(a))
    hist = lambda a: _layer(lyr, *hist_blk(a))
    act = jax.ShapeDtypeStruct((n, ch), BF16)
    same = lambda a: jax.ShapeDtypeStruct(a.shape, F32)
    stacked = {2: hist_blk(cstate), 3: hist_blk(dstate), 4: lead_blk(sstate)}
    call, extra = _stacked_call(
        _odd_sample_kernel, 18, carried, stacked, lyr,
        grid=(n // bt,),
        in_specs=[zcol(0), zcol(1), zcol(2), zcol(3), zcol(4), zcol(5),
                  pl.BlockSpec((bt, ab.shape[-1]), lambda i: (i, 0)),
                  hist(cstate), hist(dstate), lead(sstate),
                  _whole(lyr, dw), _whole(lyr, db), _whole(lyr, ln_g), _whole(lyr, ln_b),
                  _whole(lyr, conv_w), _whole(lyr, a_log), _whole(lyr, dt_bias), _whole(lyr, norm_g)],
        out_specs=[zcol(0), zcol(0), None, None, None],
        out_shape=[act, act, same(cstate), same(dstate), same(sstate)],
        scratch_shapes=[pltpu.VMEM((bt, ch), F32)],
        compiler_params=_params("parallel"),
        name="odd_sample",
    )
    return call(z, z, z, z, z, z, ab, cstate, dstate, sstate, dw, db, ln_g, ln_b, conv_w, a_log, dt_bias, norm_g,
                *extra)


def kernel(x_prompt, x_sample, state_pool, state_conv_c, state_dn_conv, state_dn_S, norm_mix, norm_ffn, norm_final, ev_w_in, pool_w, pool_scale, sgu_ln_g, sgu_ln_b, sgu_ws, sgu_b, ev_w_out, od_w_in, cv_dw, cv_db, cv_ln_g, cv_ln_b, dn_conv_w, dn_a_log, dn_dt_bias, dn_norm_g, od_w_out, ffn_w_up, ffn_w_down):
    bp, t_len, d = x_prompt.shape
    bs = x_sample.shape[0]
    assert x_sample.shape[1] == 1
    depth = norm_mix.shape[0]
    n_even, n_odd = ev_w_in.shape[0], od_w_in.shape[0]
    ch = pool_scale.shape[1]
    odd_main = 6 * ch
    n_gate = 2 * DN_HEADS
    assert od_w_in.shape[2] == odd_main + n_gate and odd_main % n_gate == 0

    rows = lambda a: a.reshape(a.shape[0], 1, a.shape[-1])
    gate_rows = lambda a: rows(jnp.pad(a, ((0, 0), (0, DN_HEADS))))
    ev_in = ev_w_in.astype(BF16)
    ev_out = ev_w_out.astype(BF16)
    od_in = jnp.swapaxes(od_w_in, 1, 2)
    od_out = od_w_out.astype(BF16)
    w_up = ffn_w_up.astype(BF16)
    w_down = ffn_w_down.astype(BF16)
    pool_wb = pool_w.astype(BF16)
    g_mix, g_ffn, g_fin = rows(norm_mix), rows(norm_ffn), norm_final.reshape(1, d)
    p_scale, s_lg, s_lb = rows(pool_scale), rows(sgu_ln_g), rows(sgu_ln_b)
    s_bt = jnp.swapaxes(sgu_b, 1, 2)
    s_w0 = rows(jnp.repeat(sgu_ws[:, :, 0, 0], ch // SGU_HEADS, axis=1))
    s_b0 = rows(jnp.repeat(sgu_b[:, :, 0], ch // SGU_HEADS, axis=1))
    c_db, c_lg, c_lb = rows(cv_db), rows(cv_ln_g), rows(cv_ln_b)
    a_log, dt_bias, d_ng = gate_rows(dn_a_log), gate_rows(dn_dt_bias), rows(dn_norm_g)

    hist_major = lambda a: jnp.swapaxes(a, 1, 2)
    st_pool, st_cc, st_dc = hist_major(state_pool), hist_major(state_conv_c), hist_major(state_dn_conv)

    xp = x_prompt.reshape(bp * t_len, d)
    xs = x_sample.reshape(bs, d)
    tn = _tiles(bp * t_len)[2]
    flat = lambda a: a.reshape(bp * t_len, a.shape[-1])
    pool_p, dc_p = [], []
    even_s = cc_p = s_p = odd_s = None
    for l in range(depth):
        i = l // 2
        if l % 2 == 0:
            zp = _norm_matmul(xp, g_mix, l, ev_in, i, 3 * ch, tn)
            zs = _norm_matmul(xs, g_mix, l, ev_in, i, 3 * ch, tn)
            zp3 = zp.reshape(bp, t_len, 3 * ch)
            a_p, b_p = _even_prompt(zp3, i, pool_wb, p_scale, s_lg, s_lb, sgu_ws, s_bt, 256)
            a_s, b_s, *even_s = _even_sample(zs, i, st_pool, pool_wb, p_scale, s_lg, s_lb, s_w0, s_b0, even_s, 8)
            pool_p.append(zp3[:, t_len - POOL_HIST:, :ch])
            xp = _mix_out(flat(a_p), flat(b_p), ev_out, i, xp)
            xs = _mix_out(a_s, b_s, ev_out, i, xs)
        else:
            zp = _norm_matmul(xp, g_mix, l, od_in, i, odd_main, tn, True)
            zs = _norm_matmul(xs, g_mix, l, od_in, i, odd_main, tn, True)
            abp = _norm_matmul(xp, g_mix, l, od_in, i, n_gate, n_gate, True, odd_main // n_gate)
            abs_ = _norm_matmul(xs, g_mix, l, od_in, i, n_gate, n_gate, True, odd_main // n_gate)
            zp3 = zp.reshape(bp, t_len, odd_main)
            c_p, *cc_p = _conv_prompt(zp3, i, n_odd, cv_dw, c_db, c_lg, c_lb, cc_p, 256)
            d_p, *s_p = _delta_prompt(zp3, 2, 5, abp.reshape(bp, t_len, n_gate), i, n_odd, dn_conv_w, a_log, dt_bias,
                                      d_ng, s_p)
            c_s, d_s, *odd_s = _odd_sample(zs, abs_, i, st_cc, st_dc, state_dn_S, cv_dw, c_db, c_lg,
                                           c_lb, dn_conv_w, a_log, dt_bias, d_ng, odd_s, 8)
            dc_p.append(zp3[:, t_len - (DN_CONV - 1):, 2 * ch:5 * ch])
            xp = _mix_out(flat(c_p), flat(d_p), od_out, i, xp)
            xs = _mix_out(c_s, d_s, od_out, i, xs)
        last = l == depth - 1
        xp = _ffn(xp, g_ffn, l, w_up, w_down, g_fin, last)
        xs = _ffn(xs, g_ffn, l, w_up, w_down, g_fin, last)
    pool_s, v_s = even_s
    cc_s, dc_s, s_s = odd_s
    return (xp.reshape(bp, t_len, d), xs.reshape(bs, 1, d), jnp.stack(pool_p), hist_major(pool_s),
            v_s.reshape(n_even, bs, 1, ch), cc_p[0], hist_major(cc_s), jnp.stack(dc_p), hist_major(dc_s), s_p[0], s_s)
```

```python
import functools

import jax
import jax.numpy as jnp
from jax import lax
from jax.experimental import pallas as pl
from jax.experimental.pallas import tpu as pltpu

F32 = jnp.float32
BF16 = jnp.bfloat16
HI = lax.Precision.HIGHEST

EPS = 1e-6
PAST_LEN = 16384
POOL_WINDOWS = (2, 4, 8, 16)
POOL_HIST = max(POOL_WINDOWS) - 1
SGU_HEADS = 4
SGU_CHUNK = 128
CONV_WIDTH = 31
DN_HEADS = 8
DN_DK = 128
DN_CONV = 4
DN_CHUNK = 64
DN_SOLVE_BLOCK = 16
DN_STEP_CHUNKS = 2

LANES = 128
SUBLANES = 8
VMEM_LIMIT = 48 * 1024 * 1024

POOL_HALO = 16
CONV_HALO = 32
DN_HALO = 8
CONV_ROWS, CONV_LANES = 64, 256


def _tiles(m):
    return min(m, 1024), min(m, 512), 512


def _params(*semantics):
    return pltpu.CompilerParams(dimension_semantics=semantics, vmem_limit_bytes=VMEM_LIMIT)


def _layer(lyr, block, index_map):
    return pl.BlockSpec((None,) + tuple(block), lambda *g: (lyr,) + tuple(index_map(*g)))


def _whole(lyr, arr):
    return pl.BlockSpec((None,) + arr.shape[1:], lambda *g: (lyr,) + (0,) * (arr.ndim - 1))


def _skip_refs(fn, start, count):
    def body(*refs):
        return fn(*refs[:start], *refs[start + count:])
    return body


def _own_layer_zero_rest(fn, n_in, outs, lyr):
    def body(*refs):
        refs = list(refs)
        for o in outs:
            full = refs[n_in + o]
            for other in range(full.shape[0]):
                if other != lyr:
                    full[other] = jnp.zeros(full.shape[1:], full.dtype)
            refs[n_in + o] = full.at[lyr]
        return fn(*refs)
    return body


def _stacked_call(kernel_fn, n_in, carried, stacked_out, lyr, **kw):
    out_specs = list(kw.pop("out_specs"))
    if carried is None:
        for o, (block, imap) in stacked_out.items():
            n_lyr = kw["out_shape"][o].shape[0]
            out_specs[o] = pl.BlockSpec((n_lyr,) + tuple(block), lambda *g, imap=imap: (0,) + tuple(imap(*g)))
        body = _own_layer_zero_rest(kernel_fn, n_in, tuple(stacked_out), lyr)
        return pl.pallas_call(body, out_specs=out_specs, **kw), ()
    for o, (block, imap) in stacked_out.items():
        out_specs[o] = _layer(lyr, block, imap)
    kw["in_specs"] = list(kw["in_specs"]) + [pl.BlockSpec(memory_space=pl.ANY)] * len(carried)
    aliases = {n_in + k: o for k, o in enumerate(stacked_out)}
    body = _skip_refs(kernel_fn, n_in, len(carried))
    return pl.pallas_call(body, out_specs=out_specs, input_output_aliases=aliases, **kw), tuple(carried)


def _rms_rows(x, g):
    return x * lax.rsqrt(jnp.mean(x * x, -1, keepdims=True) + EPS) * g


def _layernorm_rows(x, g, b):
    mu = jnp.mean(x, -1, keepdims=True)
    xc = x - mu
    var = jnp.mean(xc * xc, -1, keepdims=True)
    return xc * lax.rsqrt(var + EPS) * g + b


def _gelu(x):
    return 0.5 * x * (1.0 + lax.erf(x * 0.7071067811865476))


def _silu(x):
    return x * jax.nn.sigmoid(x)


def _dot(a, b):
    return jnp.dot(a, b, preferred_element_type=F32)


def _dot_hi(a, b):
    return jnp.dot(a, b, precision=HI, preferred_element_type=F32)


def _dot_nt_hi(a, b):
    return lax.dot_general(a, b, (((1,), (1,)), ((), ())), precision=HI, preferred_element_type=F32)


def _bdot(a, b):
    return jnp.dot(a.astype(BF16), b.astype(BF16), preferred_element_type=F32)


def _bdot_nt(a, b):
    return lax.dot_general(a.astype(BF16), b.astype(BF16), (((1,), (1,)), ((), ())), preferred_element_type=F32)


def _bdot_tn(a, b):
    return lax.dot_general(a.astype(BF16), b.astype(BF16), (((0,), (0,)), ((), ())), preferred_element_type=F32)


def _norm_matmul_kernel(x_ref, g_ref, w_ref, o_ref, h_ref, *, w_rows_out):
    @pl.when(pl.program_id(1) == 0)
    def _():
        h_ref[...] = _rms_rows(x_ref[...], g_ref[...]).astype(BF16)

    w = w_ref[...].astype(BF16)
    if w_rows_out:
        o_ref[...] = lax.dot_general(h_ref[...], w, (((1,), (1,)), ((), ())), preferred_element_type=F32)
    else:
        o_ref[...] = _dot(h_ref[...], w)


def _norm_matmul(x, gains, lyr, w, wl, n_out, tn, w_rows_out=False, col0=0):
    m, d = x.shape
    tm = _tiles(m)[0]
    if w_rows_out:
        w_spec = _layer(wl, (tn, d), lambda i, j: (col0 + j, 0))
    else:
        w_spec = _layer(wl, (d, tn), lambda i, j: (0, col0 + j))
    return pl.pallas_call(
        functools.partial(_norm_matmul_kernel, w_rows_out=w_rows_out),
        grid=(m // tm, n_out // tn),
        in_specs=[
            pl.BlockSpec((tm, d), lambda i, j: (i, 0)),
            _whole(lyr, gains),
            w_spec,
        ],
        out_specs=pl.BlockSpec((tm, tn), lambda i, j: (i, j)),
        out_shape=jax.ShapeDtypeStruct((m, n_out), F32),
        scratch_shapes=[pltpu.VMEM((tm, d), BF16)],
        compiler_params=_params("parallel", "arbitrary"),
        name="norm_matmul",
    )(x, gains, w)


def _mix_out_kernel(a0_ref, a1_ref, w0_ref, w1_ref, x_ref, o_ref):
    o_ref[...] = x_ref[...] + _dot(a0_ref[...], w0_ref[...]) + _dot(a1_ref[...], w1_ref[...])


def _mix_out(a0, a1, w, wl, x):
    m, kh = a0.shape
    n = w.shape[2]
    tm = _tiles(m)[1]
    w_half = lambda r: pl.BlockSpec((None, kh, n), lambda i: (wl, r, 0), pipeline_mode=pl.Buffered(1))
    return pl.pallas_call(
        _mix_out_kernel,
        grid=(m // tm,),
        in_specs=[
            pl.BlockSpec((tm, kh), lambda i: (i, 0)),
            pl.BlockSpec((tm, kh), lambda i: (i, 0)),
            w_half(0), w_half(1),
            pl.BlockSpec((tm, n), lambda i: (i, 0)),
        ],
        out_specs=pl.BlockSpec((tm, n), lambda i: (i, 0)),
        out_shape=jax.ShapeDtypeStruct((m, n), F32),
        compiler_params=_params("parallel"),
        name="mix_out",
    )(a0, a1, w, w, x)


def _ffn_kernel(x_ref, g_ref, wg_ref, wu_ref, wd_ref, gf_ref, o_ref, *rest, final_norm, emit_weights):
    h_ref = rest[-1]
    f = pl.program_id(1)

    @pl.when(f == 0)
    def _():
        x = x_ref[...]
        h_ref[...] = _rms_rows(x, g_ref[...]).astype(BF16)
        o_ref[...] = x

    wg, wu, wd = wg_ref[...].astype(BF16), wu_ref[...].astype(BF16), wd_ref[...].astype(BF16)
    if emit_weights:
        wg_out, wu_out, wd_out = rest[:3]
        wg_out[...] = wg
        wu_out[...] = wu
        wd_out[...] = wd
    h = h_ref[...]
    act = _silu(_dot(h, wg)) * _dot(h, wu)
    o_ref[...] += _dot(act.astype(BF16), wd)

    if final_norm:
        @pl.when(f == pl.num_programs(1) - 1)
        def _():
            o_ref[...] = _rms_rows(o_ref[...], gf_ref[...])


def _ffn(x, gains, lyr, weights, g_final, final_norm):
    m, d = x.shape
    _, tm, tf = _tiles(m)
    emit_weights = len(weights) == 2
    if emit_weights:
        assert m == tm
        w_up, w_down = weights
        d_ff = w_down.shape[1]
        nf = d_ff // tf
        operands = (w_up, w_up, w_down)
        w_specs = [_layer(lyr, (d, tf), lambda i, f: (0, f)), _layer(lyr, (d, tf), lambda i, f: (0, f + nf)),
                   _layer(lyr, (tf, d), lambda i, f: (f, 0))]
        out_specs = [pl.BlockSpec((d, tf), lambda i, f: (0, f)), pl.BlockSpec((d, tf), lambda i, f: (0, f)),
                     pl.BlockSpec((tf, d), lambda i, f: (f, 0))]
        out_shape = [jax.ShapeDtypeStruct((d, d_ff), BF16), jax.ShapeDtypeStruct((d, d_ff), BF16),
                     jax.ShapeDtypeStruct((d_ff, d), BF16)]
    else:
        operands = weights
        d_ff = weights[2].shape[0]
        nf = d_ff // tf
        w_specs = [pl.BlockSpec((d, tf), lambda i, f: (0, f)), pl.BlockSpec((d, tf), lambda i, f: (0, f)),
                   pl.BlockSpec((tf, d), lambda i, f: (f, 0))]
        out_specs, out_shape = [], []
    return pl.pallas_call(
        functools.partial(_ffn_kernel, final_norm=final_norm, emit_weights=emit_weights),
        grid=(m // tm, nf),
        in_specs=[pl.BlockSpec((tm, d), lambda i, f: (i, 0)), _whole(lyr, gains), *w_specs,
                  pl.BlockSpec((1, d), lambda i, f: (0, 0))],
        out_specs=[pl.BlockSpec((tm, d), lambda i, f: (i, 0)), *out_specs],
        out_shape=[jax.ShapeDtypeStruct((m, d), F32), *out_shape],
        scratch_shapes=[pltpu.VMEM((tm, d), BF16)],
        compiler_params=_params("parallel", "arbitrary"),
        name="ffn",
    )(x, gains, *operands, g_final)


def _even_prompt_kernel(a_ref, halo_ref, u_ref, v_ref, pw_ref, ps_ref, lg_ref, lb_ref, ws_ref, bst_ref,
                        oa_ref, ob_ref, buf_ref):
    t = pl.program_id(1)
    tt, ch = a_ref.shape
    gch = ch // len(POOL_WINDOWS)
    a = a_ref[...]
    buf_ref[0:POOL_HALO, :] = jnp.where(t > 0, halo_ref[...], 0.0)
    buf_ref[POOL_HALO:, :] = a
    pos = t * tt + lax.broadcasted_iota(jnp.int32, (tt, 1), 0)
    for gi, w in enumerate(POOL_WINDOWS):
        cols = slice(gi * gch, (gi + 1) * gch)
        s = a[:, cols]
        for k in range(1, w):
            s = s + buf_ref[POOL_HALO - k:POOL_HALO - k + tt, cols]
        cnt = jnp.minimum(w, pos + 1).astype(F32)
        pooled = s / cnt - a[:, cols]
        mixed = _dot(pooled.astype(BF16), pw_ref[gi])
        oa_ref[:, cols] = (mixed * ps_ref[:, cols]).astype(BF16)

    u = _gelu(u_ref[...])
    vn = _layernorm_rows(_gelu(v_ref[...]), lg_ref[...], lb_ref[...])
    hch = ch // SGU_HEADS
    row = lax.broadcasted_iota(jnp.int32, (SGU_CHUNK, SGU_CHUNK), 0)
    col = lax.broadcasted_iota(jnp.int32, (SGU_CHUNK, SGU_CHUNK), 1)
    for h in range(SGU_HEADS):
        cols = slice(h * hch, (h + 1) * hch)
        wm = jnp.where(row >= col, ws_ref[h], 0.0).astype(BF16)
        bias = bst_ref[:, h:h + 1]
        for c in range(tt // SGU_CHUNK):
            rows = slice(c * SGU_CHUNK, (c + 1) * SGU_CHUNK)
            mixed = _dot(wm, vn[rows, cols].astype(BF16)) + bias
            ob_ref[rows, cols] = (u[rows, cols] * mixed).astype(BF16)


def _even_prompt(z3, lyr, pool_w, pool_scale, ln_g, ln_b, ws, bst, tt):
    b, t_len, _ = z3.shape
    ch = pool_scale.shape[-1]
    hb = tt // POOL_HALO
    out = jax.ShapeDtypeStruct((b, t_len, ch), BF16)
    return pl.pallas_call(
        _even_prompt_kernel,
        grid=(b, t_len // tt),
        in_specs=[
            pl.BlockSpec((None, tt, ch), lambda i, t: (i, t, 0)),
            pl.BlockSpec((None, POOL_HALO, ch), lambda i, t: (i, jnp.maximum(t * hb - 1, 0), 0)),
            pl.BlockSpec((None, tt, ch), lambda i, t: (i, t, 1)),
            pl.BlockSpec((None, tt, ch), lambda i, t: (i, t, 2)),
            _whole(lyr, pool_w), _whole(lyr, pool_scale), _whole(lyr, ln_g), _whole(lyr, ln_b),
            _whole(lyr, ws), _whole(lyr, bst),
        ],
        out_specs=[pl.BlockSpec((None, tt, ch), lambda i, t: (i, t, 0))] * 2,
        out_shape=[out, out],
        scratch_shapes=[pltpu.VMEM((POOL_HALO + tt, ch), F32)],
        compiler_params=_params("parallel", "arbitrary"),
        name="even_prompt",
    )(z3, z3, z3, z3, pool_w, pool_scale, ln_g, ln_b, ws, bst)


def _even_sample_kernel(a_ref, u_ref, v_ref, st_ref, pw_ref, ps_ref, lg_ref, lb_ref, w0_ref, b0_ref,
                        oa_ref, ob_ref, nst_ref, nv_ref):
    ch = a_ref.shape[1]
    gch = ch // len(POOL_WINDOWS)
    a = a_ref[...]
    for gi, w in enumerate(POOL_WINDOWS):
        cols = slice(gi * gch, (gi + 1) * gch)
        s = a[:, cols]
        for k in range(1, w):
            s = s + st_ref[POOL_HIST - k, :, cols]
        pooled = s / float(min(w, PAST_LEN + 1)) - a[:, cols]
        mixed = _dot(pooled.astype(BF16), pw_ref[gi])
        oa_ref[:, cols] = (mixed * ps_ref[:, cols]).astype(BF16)
    nst_ref[0:POOL_HIST - 1] = st_ref[1:POOL_HIST]
    nst_ref[POOL_HIST - 1] = a

    u = _gelu(u_ref[...])
    vn = _layernorm_rows(_gelu(v_ref[...]), lg_ref[...], lb_ref[...])
    nv_ref[...] = vn
    ob_ref[...] = (u * (vn * w0_ref[...] + b0_ref[...])).astype(BF16)


def _even_sample(z, lyr, state, pool_w, pool_scale, ln_g, ln_b, w0, b0, carried, bt):
    assert PAST_LEN % SGU_CHUNK == 0
    n = z.shape[0]
    ch = pool_scale.shape[-1]
    act = jax.ShapeDtypeStruct((n, ch), BF16)
    rows = pl.BlockSpec((bt, ch), lambda i: (i, 0))
    n_in = 10
    stacked = {2: ((POOL_HIST, bt, ch), lambda i: (0, i, 0)), 3: ((bt, ch), lambda i: (i, 0))}
    call, extra = _stacked_call(
        _even_sample_kernel, n_in, carried, stacked, lyr,
        grid=(n // bt,),
        in_specs=[
            rows, pl.BlockSpec((bt, ch), lambda i: (i, 1)), pl.BlockSpec((bt, ch), lambda i: (i, 2)),
            _layer(lyr, (POOL_HIST, bt, ch), lambda i: (0, i, 0)),
            _whole(lyr, pool_w), _whole(lyr, pool_scale), _whole(lyr, ln_g), _whole(lyr, ln_b),
            _whole(lyr, w0), _whole(lyr, b0),
        ],
        out_specs=[rows, rows, None, None],
        out_shape=[act, act, jax.ShapeDtypeStruct(state.shape, F32),
                   jax.ShapeDtypeStruct((state.shape[0], n, ch), F32)],
        compiler_params=_params("parallel"),
        name="even_sample",
    )
    return call(z, z, z, state, pool_w, pool_scale, ln_g, ln_b, w0, b0, *extra)


def _conv_prompt_kernel(p_ref, q_ref, hp_ref, hq_ref, dw_ref, db_ref, lg_ref, lb_ref, o_ref, st_ref, buf_ref,
                        rot_ref, y_ref, wb_ref):
    t = pl.program_id(1)
    tt = p_ref.shape[0]
    hist = CONV_WIDTH - 1
    glu = p_ref[...] * jax.nn.sigmoid(q_ref[...])
    halo = hp_ref[...] * jax.nn.sigmoid(hq_ref[...])
    buf_ref[0:CONV_HALO, :] = jnp.where(t > 0, halo, 0.0)
    buf_ref[CONV_HALO:, :] = glu
    span = rot_ref.shape[1]
    for r in range(1, SUBLANES):
        rot_ref[r - 1] = buf_ref[r:r + span, :]
    base = CONV_HALO - hist
    for k in range(CONV_WIDTH):
        wb_ref[k] = jnp.broadcast_to(dw_ref[k:k + 1, :], wb_ref.shape[1:])
    subs = range(CONV_ROWS // SUBLANES)
    lane_chunks = p_ref.shape[1] // CONV_LANES

    def chunk(i, carry):
        r0 = pl.multiple_of((i // lane_chunks) * CONV_ROWS, CONV_ROWS)
        lanes = pl.ds(pl.multiple_of((i % lane_chunks) * CONV_LANES, CONV_LANES), CONV_LANES)
        at = lambda src, off, j: src[pl.ds(off + r0 + j * SUBLANES, SUBLANES), lanes]
        bias = db_ref[:, lanes]
        w = wb_ref[hist, :, lanes]
        acc = [bias + w * at(buf_ref, CONV_HALO, j) for j in subs]
        for k in range(hist):
            r = (base + k) % SUBLANES
            a = base + k - r
            src = buf_ref if r == 0 else rot_ref.at[r - 1]
            w = wb_ref[k, :, lanes]
            acc = [acc[j] + w * at(src, a, j) for j in subs]
        for j in subs:
            y_ref[pl.ds(r0 + j * SUBLANES, SUBLANES), lanes] = acc[j]
        return carry

    lax.fori_loop(0, (tt // CONV_ROWS) * lane_chunks, chunk, 0, unroll=2)
    o_ref[...] = _silu(_layernorm_rows(y_ref[...], lg_ref[...], lb_ref[...])).astype(BF16)

    @pl.when(t == pl.num_programs(1) - 1)
    def _():
        st_ref[...] = buf_ref[CONV_HALO + tt - hist:CONV_HALO + tt, :]


def _conv_prompt(z3, lyr, n_lyr, dw, db, ln_g, ln_b, carried, tt):
    b, t_len, _ = z3.shape
    ch = db.shape[-1]
    hb = tt // CONV_HALO
    hist = CONV_WIDTH - 1
    halo = lambda c: pl.BlockSpec((None, CONV_HALO, ch), lambda i, t: (i, jnp.maximum(t * hb - 1, 0), c))
    stacked = {1: ((None, hist, ch), lambda i, t: (i, 0, 0))}
    call, extra = _stacked_call(
        _conv_prompt_kernel, 8, carried, stacked, lyr,
        grid=(b, t_len // tt),
        in_specs=[
            pl.BlockSpec((None, tt, ch), lambda i, t: (i, t, 0)),
            pl.BlockSpec((None, tt, ch), lambda i, t: (i, t, 1)),
            halo(0), halo(1),
            _whole(lyr, dw), _whole(lyr, db), _whole(lyr, ln_g), _whole(lyr, ln_b),
        ],
        out_specs=[
            pl.BlockSpec((None, tt, ch), lambda i, t: (i, t, 0)),
            None,
        ],
        out_shape=[jax.ShapeDtypeStruct((b, t_len, ch), BF16), jax.ShapeDtypeStruct((n_lyr, b, hist, ch), F32)],
        scratch_shapes=[pltpu.VMEM((CONV_HALO + tt, ch), F32),
                        pltpu.VMEM((SUBLANES - 1, CONV_HALO + tt - SUBLANES, ch), F32),
                        pltpu.VMEM((tt, ch), F32), pltpu.VMEM((CONV_WIDTH, SUBLANES, ch), F32)],
        compiler_params=_params("parallel", "arbitrary"),
        name="conv_prompt",
    )
    return call(z3, z3, z3, z3, dw, db, ln_g, ln_b, *extra)


def _l2norm_heads(x, scale):
    outs = []
    for h in range(x.shape[1] // DN_DK):
        xh = x[:, h * DN_DK:(h + 1) * DN_DK]
        outs.append(xh * (lax.rsqrt(jnp.sum(xh * xh, -1, keepdims=True) + EPS) * scale))
    return jnp.concatenate(outs, -1)


def _dn_short_conv(raw_refs, w_ref, first, tail_ref, buf_ref, out_ref):
    hist = DN_CONV - 1
    base = DN_HALO - hist
    for part, (x_ref, scale) in enumerate(zip(raw_refs, (DN_DK ** -0.5, 1.0, None))):
        tt, ch = x_ref.shape
        cols = slice(part * ch, (part + 1) * ch)
        x = x_ref[...]
        buf_ref[0:DN_HALO, :] = jnp.where(first, 0.0, tail_ref[part])
        buf_ref[DN_HALO:, :] = x
        tail_ref[part] = x[tt - DN_HALO:, :]
        acc = w_ref[hist:hist + 1, cols] * x
        for k in range(hist):
            acc = acc + w_ref[k:k + 1, cols] * buf_ref[base + k:base + k + tt, :]
        y = _silu(acc)
        out_ref[part] = y if scale is None else _l2norm_heads(y, scale)


def _gate_terms(ab, alog, dtb):
    g = -jnp.exp(alog) * jax.nn.softplus(ab + dtb)
    return g, jax.nn.sigmoid(ab)


def _each(fn, *lists):
    return [fn(*xs) for xs in zip(*lists)]


def _delta_chunk_terms(q, k, v, gi, gj, bi, masks):
    lower, strict, same_block = masks
    n = q[0].shape[0]
    decay = _each(lambda a, b: jnp.where(lower, jnp.exp(jnp.where(lower, a - b, 0.0)), 0.0), gi, gj)
    eg = _each(jnp.exp, gi)
    kb = _each(jnp.multiply, k, bi)
    kk = _each(_bdot_nt, kb, k)
    amat = _each(lambda a, d: jnp.where(strict, a * d, 0.0), kk, decay)
    ad = _each(lambda a: jnp.where(same_block, a, 0.0), amat)
    an = _each(jnp.subtract, amat, ad)
    doff = _each(jnp.negative, ad)
    pw = ad
    for _ in range(DN_SOLVE_BLOCK.bit_length() - 2):
        pw = _each(_bdot, pw, pw)
        dp = _each(_bdot, doff, pw)
        doff = _each(lambda d, p, x: d + p + x, doff, pw, dp)
    assert n // DN_SOLVE_BLOCK == 4
    bm = _each(jnp.add, an, _each(_bdot, doff, an))
    rhs = _each(lambda vv, b, kk_, e: jnp.concatenate([vv * b, kk_ * e], -1), v, bi, kb, eg)
    sol = _each(jnp.add, rhs, _each(_bdot, doff, rhs))
    b2 = _each(_bdot, bm, bm)
    sol = _each(jnp.add, sol, _each(_bdot, b2, sol))
    sol = _each(jnp.subtract, sol, _each(_bdot, bm, sol))
    qk = _each(jnp.multiply, _each(_bdot_nt, q, k), decay)
    g_last = _each(lambda a: a[n - 1:n, :], gi)
    return (_each(lambda x: x[:, :DN_DK], sol), _each(lambda x: x[:, DN_DK:], sol), qk, _each(jnp.multiply, q, eg),
            _each(lambda kk_, gl, a: kk_ * jnp.exp(gl - a), k, g_last, gi), _each(jnp.exp, g_last))


def _delta_prompt_kernel(q_raw, k_raw, v_raw, zg_ref, ab_ref, cw_ref, alog_ref, dtb_ref, ng_ref, o_ref, sfin_ref,
                         s_ref, tail_ref, buf_ref, qkv_ref):
    c = pl.program_id(1)
    n = DN_CHUNK
    heads = range(DN_HEADS)
    chunks = range(q_raw.shape[0] // n)

    @pl.when(c == 0)
    def _():
        s_ref[...] = jnp.zeros_like(s_ref)

    _dn_short_conv((q_raw, k_raw, v_raw), cw_ref, c == 0, tail_ref, buf_ref, qkv_ref)
    q_ref, k_ref, v_ref = qkv_ref.at[0], qkv_ref.at[1], qkv_ref.at[2]

    row = lax.broadcasted_iota(jnp.int32, (n, n), 0)
    col = lax.broadcasted_iota(jnp.int32, (n, n), 1)
    lower = row >= col
    masks = (lower, row > col, (row // DN_SOLVE_BLOCK) == (col // DN_SOLVE_BLOCK))
    tri = lower.astype(F32)
    rows = [slice(j * n, (j + 1) * n) for j in chunks]
    cols = [slice(h * DN_DK, (h + 1) * DN_DK) for h in heads]
    gate = [_gate_terms(ab_ref[r, :], alog_ref[...], dtb_ref[...]) for r in rows]
    gcol = [_dot_hi(tri, g) for g, _ in gate]
    grow = [_dot_nt_hi(g.T, tri) for g, _ in gate]
    units = [(j, h) for j in chunks for h in heads]
    u, w, qk, qd, kd, gl = _delta_chunk_terms(
        [q_ref[rows[j], cols[h]] for j, h in units], [k_ref[rows[j], cols[h]] for j, h in units],
        [v_ref[rows[j], cols[h]] for j, h in units], [gcol[j][:, h:h + 1] for j, h in units],
        [grow[j][h:h + 1, :] for j, h in units],
        [gate[j][1][:, DN_HEADS + h:DN_HEADS + h + 1] for j, h in units], masks)
    s = [s_ref[h] for h in heads]
    for j in chunks:
        at = lambda xs: xs[j * DN_HEADS:(j + 1) * DN_HEADS]
        v_new = _each(jnp.subtract, at(u), _each(_bdot, at(w), s))
        o = _each(jnp.add, _each(_bdot, at(qd), s), _each(_bdot, at(qk), v_new))
        s = _each(lambda ss, g, x: ss * g + x, s, at(gl), _each(_bdot_tn, at(kd), v_new))
        for h in heads:
            on = o[h] * lax.rsqrt(jnp.mean(o[h] * o[h], -1, keepdims=True) + EPS) * ng_ref[...]
            o_ref[rows[j], cols[h]] = (on * _silu(zg_ref[rows[j], cols[h]])).astype(BF16)
    for h in heads:
        s_ref[h] = s[h]

    @pl.when(c == pl.num_programs(1) - 1)
    def _():
        sfin_ref[...] = s_ref[...]


def _delta_prompt(z3, qkv_col, zg_col, ab3, lyr, n_lyr, conv_w, a_log, dt_bias, norm_g, carried):
    b, t_len, _ = z3.shape
    ch = DN_HEADS * DN_DK
    n = DN_CHUNK * DN_STEP_CHUNKS
    zcol = lambda col: pl.BlockSpec((None, n, ch), lambda i, c: (i, c, col))
    stacked = {1: ((None, DN_HEADS, DN_DK, DN_DK), lambda i, c: (i, 0, 0, 0))}
    call, extra = _stacked_call(
        _delta_prompt_kernel, 9, carried, stacked, lyr,
        grid=(b, t_len // n),
        in_specs=[
            zcol(qkv_col), zcol(qkv_col + 1), zcol(qkv_col + 2), zcol(zg_col),
            pl.BlockSpec((None, n, ab3.shape[-1]), lambda i, c: (i, c, 0)),
            _whole(lyr, conv_w), _whole(lyr, a_log), _whole(lyr, dt_bias), _whole(lyr, norm_g),
        ],
        out_specs=[
            pl.BlockSpec((None, n, ch), lambda i, c: (i, c, 0)),
            None,
        ],
        out_shape=[jax.ShapeDtypeStruct((b, t_len, ch), BF16),
                   jax.ShapeDtypeStruct((n_lyr, b, DN_HEADS, DN_DK, DN_DK), F32)],
        scratch_shapes=[pltpu.VMEM((DN_HEADS, DN_DK, DN_DK), F32), pltpu.VMEM((3, DN_HALO, ch), F32),
                        pltpu.VMEM((DN_HALO + n, ch), F32), pltpu.VMEM((3, n, ch), F32)],
        compiler_params=_params("parallel", "arbitrary"),
        name="delta_prompt",
    )
    return call(z3, z3, z3, z3, ab3, conv_w, a_log, dt_bias, norm_g, *extra)


def _odd_sample_kernel(p_ref, gt_ref, q_ref, k_ref, v_ref, zg_ref, ab_ref, cst_ref, dst_ref, s_ref,
                       dw_ref, db_ref, lg_ref, lb_ref, cw_ref, alog_ref, dtb_ref, ng_ref,
                       oc_ref, od_ref, ncst_ref, ndst_ref, ns_ref, obuf_ref):
    bt, ch = p_ref.shape
    hist = CONV_WIDTH - 1
    glu = p_ref[...] * jax.nn.sigmoid(gt_ref[...])
    acc = db_ref[...] + dw_ref[hist:hist + 1, :] * glu
    for k in range(hist):
        acc = acc + dw_ref[k:k + 1, :] * cst_ref[k]
    oc_ref[...] = _silu(_layernorm_rows(acc, lg_ref[...], lb_ref[...])).astype(BF16)
    ncst_ref[0:hist - 1] = cst_ref[1:hist]
    ncst_ref[hist - 1] = glu

    dh = DN_CONV - 1
    parts = []
    for pi, x_ref in enumerate((q_ref, k_ref, v_ref)):
        cols = slice(pi * ch, (pi + 1) * ch)
        x = x_ref[...]
        y = cw_ref[dh:dh + 1, cols] * x
        for k in range(dh):
            y = y + cw_ref[k:k + 1, cols] * dst_ref[k, :, cols]
        parts.append(_silu(y))
        ndst_ref[0:dh - 1, :, cols] = dst_ref[1:dh, :, cols]
        ndst_ref[dh - 1, :, cols] = x
    q = _l2norm_heads(parts[0], DN_DK ** -0.5)
    k = _l2norm_heads(parts[1], 1.0)
    v = parts[2]
    g, beta = _gate_terms(ab_ref[...], alog_ref[...], dtb_ref[...])
    eg_all = jnp.exp(g)

    assert 2 * DN_HEADS * bt == LANES
    kq = jnp.concatenate([x[:, h * DN_DK:(h + 1) * DN_DK] for x in (k, q) for h in range(DN_HEADS)], 0)
    kq_t = kq.T
    for h in range(DN_HEADS):
        cols = slice(h * DN_DK, (h + 1) * DN_DK)
        for b in range(bt):
            kc = kq_t[:, h * bt + b:h * bt + b + 1]
            qc = kq_t[:, (DN_HEADS + h) * bt + b:(DN_HEADS + h) * bt + b + 1]
            s = s_ref[b, h]
            sk = jnp.sum(s * kc, 0, keepdims=True)
            sq = jnp.sum(s * qc, 0, keepdims=True)
            eg = eg_all[b:b + 1, h:h + 1]
            bi = beta[b:b + 1, DN_HEADS + h:DN_HEADS + h + 1]
            v_new = bi * (v[b:b + 1, cols] - eg * sk)
            qk = jnp.sum(q[b:b + 1, cols] * k[b:b + 1, cols], -1, keepdims=True)
            obuf_ref[b:b + 1, cols] = eg * sq + qk * v_new
            ns_ref[b, h] = s * eg + kc * v_new
    for h in range(DN_HEADS):
        cols = slice(h * DN_DK, (h + 1) * DN_DK)
        o = obuf_ref[:, cols]
        o = o * lax.rsqrt(jnp.mean(o * o, -1, keepdims=True) + EPS) * ng_ref[...]
        od_ref[:, cols] = (o * _silu(zg_ref[:, cols])).astype(BF16)


def _odd_sample(z, ab, lyr, cstate, dstate, sstate, dw, db, ln_g, ln_b, conv_w, a_log, dt_bias, norm_g, carried, bt):
    n = z.shape[0]
    ch = db.shape[-1]
    zcol = lambda c: pl.BlockSpec((bt, ch), lambda i: (i, c))
    lead_blk = lambda a: ((bt,) + a.shape[2:], lambda i: (i,) + (0,) * (a.ndim - 2))
    hist_blk = lambda a: ((a.shape[1], bt, a.shape[3]), lambda i: (0, i, 0))
    lead = lambda a: _layer(lyr, *lead_blk(a))
    hist = lambda a: _layer(lyr, *hist_blk(a))
    act = jax.ShapeDtypeStruct((n, ch), BF16)
    same = lambda a: jax.ShapeDtypeStruct(a.shape, F32)
    stacked = {2: hist_blk(cstate), 3: hist_blk(dstate), 4: lead_blk(sstate)}
    call, extra = _stacked_call(
        _odd_sample_kernel, 18, carried, stacked, lyr,
        grid=(n // bt,),
        in_specs=[zcol(0), zcol(1), zcol(2), zcol(3), zcol(4), zcol(5),
                  pl.BlockSpec((bt, ab.shape[-1]), lambda i: (i, 0)),
                  hist(cstate), hist(dstate), lead(sstate),
                  _whole(lyr, dw), _whole(lyr, db), _whole(lyr, ln_g), _whole(lyr, ln_b),
                  _whole(lyr, conv_w), _whole(lyr, a_log), _whole(lyr, dt_bias), _whole(lyr, norm_g)],
        out_specs=[zcol(0), zcol(0), None, None, None],
        out_shape=[act, act, same(cstate), same(dstate), same(sstate)],
        scratch_shapes=[pltpu.VMEM((bt, ch), F32)],
        compiler_params=_params("parallel"),
        name="odd_sample",
    )
    return call(z, z, z, z, z, z, ab, cstate, dstate, sstate, dw, db, ln_g, ln_b, conv_w, a_log, dt_bias, norm_g,
                *extra)


def kernel(x_prompt, x_sample, state_pool, state_conv_c, state_dn_conv, state_dn_S, norm_mix, norm_ffn, norm_final, ev_w_in, pool_w, pool_scale, sgu_ln_g, sgu_ln_b, sgu_ws, sgu_b, ev_w_out, od_w_in, cv_dw, cv_db, cv_ln_g, cv_ln_b, dn_conv_w, dn_a_log, dn_dt_bias, dn_norm_g, od_w_out, ffn_w_up, ffn_w_down):
    bp, t_len, d = x_prompt.shape
    bs = x_sample.shape[0]
    assert x_sample.shape[1] == 1
    depth = norm_mix.shape[0]
    n_even, n_odd = ev_w_in.shape[0], od_w_in.shape[0]
    ch = pool_scale.shape[1]
    odd_main = 6 * ch
    n_gate = 2 * DN_HEADS
    assert od_w_in.shape[2] == odd_main + n_gate and odd_main % n_gate == 0

    rows = lambda a: a.reshape(a.shape[0], 1, a.shape[-1])
    gate_rows = lambda a: rows(jnp.pad(a, ((0, 0), (0, DN_HEADS))))
    ev_in = ev_w_in.astype(BF16)
    ev_out = ev_w_out.astype(BF16)
    od_in = jnp.swapaxes(od_w_in, 1, 2)
    od_out = od_w_out.astype(BF16)
    pool_wb = pool_w.astype(BF16)
    g_mix, g_ffn, g_fin = rows(norm_mix), rows(norm_ffn), norm_final.reshape(1, d)
    p_scale, s_lg, s_lb = rows(pool_scale), rows(sgu_ln_g), rows(sgu_ln_b)
    s_bt = jnp.swapaxes(sgu_b, 1, 2)
    s_w0 = rows(jnp.repeat(sgu_ws[:, :, 0, 0], ch // SGU_HEADS, axis=1))
    s_b0 = rows(jnp.repeat(sgu_b[:, :, 0], ch // SGU_HEADS, axis=1))
    c_db, c_lg, c_lb = rows(cv_db), rows(cv_ln_g), rows(cv_ln_b)
    a_log, dt_bias, d_ng = gate_rows(dn_a_log), gate_rows(dn_dt_bias), rows(dn_norm_g)

    hist_major = lambda a: jnp.swapaxes(a, 1, 2)
    st_pool, st_cc, st_dc = hist_major(state_pool), hist_major(state_conv_c), hist_major(state_dn_conv)

    xp = x_prompt.reshape(bp * t_len, d)
    xs = x_sample.reshape(bs, d)
    tn = _tiles(bp * t_len)[2]
    flat = lambda a: a.reshape(bp * t_len, a.shape[-1])
    pool_p, dc_p = [], []
    even_s = cc_p = s_p = odd_s = None
    for l in range(depth):
        i = l // 2
        if l % 2 == 0:
            zp = _norm_matmul(xp, g_mix, l, ev_in, i, 3 * ch, tn)
            zs = _norm_matmul(xs, g_mix, l, ev_in, i, 3 * ch, tn)
            zp3 = zp.reshape(bp, t_len, 3 * ch)
            a_p, b_p = _even_prompt(zp3, i, pool_wb, p_scale, s_lg, s_lb, sgu_ws, s_bt, 256)
            a_s, b_s, *even_s = _even_sample(zs, i, st_pool, pool_wb, p_scale, s_lg, s_lb, s_w0, s_b0, even_s, 8)
            pool_p.append(zp3[:, t_len - POOL_HIST:, :ch])
            xp = _mix_out(flat(a_p), flat(b_p), ev_out, i, xp)
            xs = _mix_out(a_s, b_s, ev_out, i, xs)
        else:
            zp = _norm_matmul(xp, g_mix, l, od_in, i, odd_main, tn, True)
            zs = _norm_matmul(xs, g_mix, l, od_in, i, odd_main, tn, True)
            abp = _norm_matmul(xp, g_mix, l, od_in, i, n_gate, n_gate, True, odd_main // n_gate)
            abs_ = _norm_matmul(xs, g_mix, l, od_in, i, n_gate, n_gate, True, odd_main // n_gate)
            zp3 = zp.reshape(bp, t_len, odd_main)
            c_p, *cc_p = _conv_prompt(zp3, i, n_odd, cv_dw, c_db, c_lg, c_lb, cc_p, 256)
            d_p, *s_p = _delta_prompt(zp3, 2, 5, abp.reshape(bp, t_len, n_gate), i, n_odd, dn_conv_w, a_log, dt_bias,
                                      d_ng, s_p)
            c_s, d_s, *odd_s = _odd_sample(zs, abs_, i, st_cc, st_dc, state_dn_S, cv_dw, c_db, c_lg,
                                           c_lb, dn_conv_w, a_log, dt_bias, d_ng, odd_s, 8)
            dc_p.append(zp3[:, t_len - (DN_CONV - 1):, 2 * ch:5 * ch])
            xp = _mix_out(flat(c_p), flat(d_p), od_out, i, xp)
            xs = _mix_out(c_s, d_s, od_out, i, xs)
        last = l == depth - 1
        xs, *w_ffn = _ffn(xs, g_ffn, l, (ffn_w_up, ffn_w_down), g_fin, last)
        xp, = _ffn(xp, g_ffn, l, w_ffn, g_fin, last)
    pool_s, v_s = even_s
    cc_s, dc_s, s_s = odd_s
    return (xp.reshape(bp, t_len, d), xs.reshape(bs, 1, d), jnp.stack(pool_p), hist_major(pool_s),
            v_s.reshape(n_even, bs, 1, ch), cc_p[0], hist_major(cc_s), jnp.stack(dc_p), hist_major(dc_s), s_p[0], s_s)
```

```python
import functools

import jax
import jax.numpy as jnp
from jax import lax
from jax.experimental import pallas as pl
from jax.experimental.pallas import tpu as pltpu

F32 = jnp.float32
BF16 = jnp.bfloat16
HI = lax.Precision.HIGHEST

EPS = 1e-6
PAST_LEN = 16384
POOL_WINDOWS = (2, 4, 8, 16)
POOL_HIST = max(POOL_WINDOWS) - 1
SGU_HEADS = 4
SGU_CHUNK = 128
CONV_WIDTH = 31
DN_HEADS = 8
DN_DK = 128
DN_CONV = 4
DN_CHUNK = 64
DN_SOLVE_BLOCK = 16
DN_STEP_CHUNKS = 4
DN_PREP_CHUNKS = 2

LANES = 128
SUBLANES = 8
VMEM_LIMIT = 48 * 1024 * 1024

POOL_HALO = 16
CONV_HALO = 32
DN_HALO = 8
CONV_ROWS, CONV_LANES = 64, 256


def _tiles(m):
    return min(m, 1024), min(m, 512), 512


def _params(*semantics):
    return pltpu.CompilerParams(dimension_semantics=semantics, vmem_limit_bytes=VMEM_LIMIT)


def _layer(lyr, block, index_map):
    return pl.BlockSpec((None,) + tuple(block), lambda *g: (lyr,) + tuple(index_map(*g)))


def _whole(lyr, arr):
    return pl.BlockSpec((None,) + arr.shape[1:], lambda *g: (lyr,) + (0,) * (arr.ndim - 1))


def _skip_refs(fn, start, count):
    def body(*refs):
        return fn(*refs[:start], *refs[start + count:])
    return body


def _own_layer_zero_rest(fn, n_in, outs, lyr):
    def body(*refs):
        refs = list(refs)
        for o in outs:
            full = refs[n_in + o]
            for other in range(full.shape[0]):
                if other != lyr:
                    full[other] = jnp.zeros(full.shape[1:], full.dtype)
            refs[n_in + o] = full.at[lyr]
        return fn(*refs)
    return body


def _stacked_call(kernel_fn, n_in, carried, stacked_out, lyr, **kw):
    out_specs = list(kw.pop("out_specs"))
    if carried is None:
        for o, (block, imap) in stacked_out.items():
            n_lyr = kw["out_shape"][o].shape[0]
            out_specs[o] = pl.BlockSpec((n_lyr,) + tuple(block), lambda *g, imap=imap: (0,) + tuple(imap(*g)))
        body = _own_layer_zero_rest(kernel_fn, n_in, tuple(stacked_out), lyr)
        return pl.pallas_call(body, out_specs=out_specs, **kw), ()
    for o, (block, imap) in stacked_out.items():
        out_specs[o] = _layer(lyr, block, imap)
    kw["in_specs"] = list(kw["in_specs"]) + [pl.BlockSpec(memory_space=pl.ANY)] * len(carried)
    aliases = {n_in + k: o for k, o in enumerate(stacked_out)}
    body = _skip_refs(kernel_fn, n_in, len(carried))
    return pl.pallas_call(body, out_specs=out_specs, input_output_aliases=aliases, **kw), tuple(carried)


def _rms_rows(x, g):
    return x * lax.rsqrt(jnp.mean(x * x, -1, keepdims=True) + EPS) * g


def _layernorm_rows(x, g, b):
    mu = jnp.mean(x, -1, keepdims=True)
    xc = x - mu
    var = jnp.mean(xc * xc, -1, keepdims=True)
    return xc * lax.rsqrt(var + EPS) * g + b


def _gelu(x):
    return 0.5 * x * (1.0 + lax.erf(x * 0.7071067811865476))


def _silu(x):
    return x * jax.nn.sigmoid(x)


def _dot(a, b):
    return jnp.dot(a, b, preferred_element_type=F32)


def _dot_hi(a, b):
    return jnp.dot(a, b, precision=HI, preferred_element_type=F32)


def _dot_nt_hi(a, b):
    return lax.dot_general(a, b, (((1,), (1,)), ((), ())), precision=HI, preferred_element_type=F32)


def _bdot(a, b):
    return jnp.dot(a.astype(BF16), b.astype(BF16), preferred_element_type=F32)


def _bdot_nt(a, b):
    return lax.dot_general(a.astype(BF16), b.astype(BF16), (((1,), (1,)), ((), ())), preferred_element_type=F32)


def _bdot_tn(a, b):
    return lax.dot_general(a.astype(BF16), b.astype(BF16), (((0,), (0,)), ((), ())), preferred_element_type=F32)


def _dot_nt(a, b):
    return lax.dot_general(a, b, (((1,), (1,)), ((), ())), preferred_element_type=F32)


def _norm_matmul_kernel(x_ref, g_ref, w_ref, *rest, w_rows_out, n_tail, emit_weights):
    o_ref, h_ref = (rest[0], rest[-1]) if not n_tail else (rest[1], rest[3])

    @pl.when(pl.program_id(1) == 0)
    def _():
        h = _rms_rows(x_ref[...], g_ref[...]).astype(BF16)
        h_ref[...] = h
        if n_tail:
            wt_ref, tail_ref = rest[0], rest[2]
            tail_ref[...] = _dot_nt(h, wt_ref[...].astype(BF16))

    w = w_ref[...].astype(BF16)
    if emit_weights:
        rest[1][...] = w
    o_ref[...] = _dot_nt(h_ref[...], w) if w_rows_out else _dot(h_ref[...], w)


def _norm_matmul(x, gains, lyr, w, wl, n_out, tn, w_rows_out=False, n_tail=0, emit_weights=False):
    m, d = x.shape
    tm = _tiles(m)[0]
    if w_rows_out:
        w_spec = _layer(wl, (tn, d), lambda i, j: (j, 0))
    else:
        assert not n_tail
        w_spec = _layer(wl, (d, tn), lambda i, j: (0, j))
    in_specs = [pl.BlockSpec((tm, d), lambda i, j: (i, 0)), _whole(lyr, gains), w_spec]
    out_specs = [pl.BlockSpec((tm, tn), lambda i, j: (i, j))]
    out_shape = [jax.ShapeDtypeStruct((m, n_out), F32)]
    operands = [x, gains, w]
    if n_tail:
        assert n_out % n_tail == 0
        in_specs.append(_layer(wl, (n_tail, d), lambda i, j: (n_out // n_tail, 0)))
        out_specs.append(pl.BlockSpec((tm, n_tail), lambda i, j: (i, 0)))
        out_shape.append(jax.ShapeDtypeStruct((m, n_tail), F32))
        operands.append(w)
    if emit_weights:
        assert m == tm and not w_rows_out and not n_tail
        out_specs.append(pl.BlockSpec((None, d, tn), lambda i, j: (0, 0, j)))
        out_shape.append(jax.ShapeDtypeStruct((1, d, n_out), BF16))
    return pl.pallas_call(
        functools.partial(_norm_matmul_kernel, w_rows_out=w_rows_out, n_tail=n_tail, emit_weights=emit_weights),
        grid=(m // tm, n_out // tn),
        in_specs=in_specs,
        out_specs=out_specs,
        out_shape=out_shape,
        scratch_shapes=[pltpu.VMEM((tm, d), BF16)],
        compiler_params=_params("parallel", "arbitrary"),
        name="norm_matmul",
    )(*operands)


def _mix_out_kernel(a0_ref, a1_ref, w0_ref, w1_ref, x_ref, o_ref, *w_out):
    w0, w1 = w0_ref[...].astype(BF16), w1_ref[...].astype(BF16)
    if w_out:
        w_out[0][0], w_out[0][1] = w0, w1
    o_ref[...] = x_ref[...] + _dot(a0_ref[...], w0) + _dot(a1_ref[...], w1)


def _mix_out(a0, a1, w, wl, x, emit_weights=False):
    m, kh = a0.shape
    n = w.shape[2]
    tm = _tiles(m)[1]
    assert not emit_weights or m == tm
    w_half = lambda r: pl.BlockSpec((None, kh, n), lambda i: (wl, r, 0), pipeline_mode=pl.Buffered(1))
    rows = pl.BlockSpec((tm, n), lambda i: (i, 0))
    out = pl.pallas_call(
        _mix_out_kernel,
        grid=(m // tm,),
        in_specs=[
            pl.BlockSpec((tm, kh), lambda i: (i, 0)),
            pl.BlockSpec((tm, kh), lambda i: (i, 0)),
            w_half(0), w_half(1),
            rows,
        ],
        out_specs=[rows] + [pl.BlockSpec((2, kh, n), lambda i: (0, 0, 0))] * emit_weights,
        out_shape=[jax.ShapeDtypeStruct((m, n), F32)] + [jax.ShapeDtypeStruct((2, kh, n), BF16)] * emit_weights,
        compiler_params=_params("parallel"),
        name="mix_out",
    )(a0, a1, w, w, x)
    return (out[0], out[1].reshape(1, 2 * kh, n)) if emit_weights else out[0]


def _ffn_kernel(x_ref, g_ref, wg_ref, wu_ref, wd_ref, gf_ref, o_ref, *rest, final_norm, emit_weights):
    h_ref = rest[-1]
    f = pl.program_id(1)

    @pl.when(f == 0)
    def _():
        x = x_ref[...]
        h_ref[...] = _rms_rows(x, g_ref[...]).astype(BF16)
        o_ref[...] = x

    wg, wu, wd = wg_ref[...].astype(BF16), wu_ref[...].astype(BF16), wd_ref[...].astype(BF16)
    if emit_weights:
        wg_out, wu_out, wd_out = rest[:3]
        wg_out[...] = wg
        wu_out[...] = wu
        wd_out[...] = wd
    h = h_ref[...]
    act = _silu(_dot(h, wg)) * _dot(h, wu)
    o_ref[...] += _dot(act.astype(BF16), wd)

    if final_norm:
        @pl.when(f == pl.num_programs(1) - 1)
        def _():
            o_ref[...] = _rms_rows(o_ref[...], gf_ref[...])


def _ffn(x, gains, lyr, weights, g_final, final_norm):
    m, d = x.shape
    _, tm, tf = _tiles(m)
    emit_weights = len(weights) == 2
    if emit_weights:
        assert m == tm
        w_up, w_down = weights
        d_ff = w_down.shape[1]
        nf = d_ff // tf
        operands = (w_up, w_up, w_down)
        w_specs = [_layer(lyr, (d, tf), lambda i, f: (0, f)), _layer(lyr, (d, tf), lambda i, f: (0, f + nf)),
                   _layer(lyr, (tf, d), lambda i, f: (f, 0))]
        out_specs = [pl.BlockSpec((d, tf), lambda i, f: (0, f)), pl.BlockSpec((d, tf), lambda i, f: (0, f)),
                     pl.BlockSpec((tf, d), lambda i, f: (f, 0))]
        out_shape = [jax.ShapeDtypeStruct((d, d_ff), BF16), jax.ShapeDtypeStruct((d, d_ff), BF16),
                     jax.ShapeDtypeStruct((d_ff, d), BF16)]
    else:
        operands = weights
        d_ff = weights[2].shape[0]
        nf = d_ff // tf
        w_specs = [pl.BlockSpec((d, tf), lambda i, f: (0, f)), pl.BlockSpec((d, tf), lambda i, f: (0, f)),
                   pl.BlockSpec((tf, d), lambda i, f: (f, 0))]
        out_specs, out_shape = [], []
    return pl.pallas_call(
        functools.partial(_ffn_kernel, final_norm=final_norm, emit_weights=emit_weights),
        grid=(m // tm, nf),
        in_specs=[pl.BlockSpec((tm, d), lambda i, f: (i, 0)), _whole(lyr, gains), *w_specs,
                  pl.BlockSpec((1, d), lambda i, f: (0, 0))],
        out_specs=[pl.BlockSpec((tm, d), lambda i, f: (i, 0)), *out_specs],
        out_shape=[jax.ShapeDtypeStruct((m, d), F32), *out_shape],
        scratch_shapes=[pltpu.VMEM((tm, d), BF16)],
        compiler_params=_params("parallel", "arbitrary"),
        name="ffn",
    )(x, gains, *operands, g_final)


def _even_prompt_kernel(a_ref, halo_ref, u_ref, v_ref, pw_ref, ps_ref, lg_ref, lb_ref, ws_ref, bst_ref,
                        oa_ref, ob_ref, buf_ref):
    t = pl.program_id(1)
    tt, ch = a_ref.shape
    gch = ch // len(POOL_WINDOWS)
    a = a_ref[...]
    buf_ref[0:POOL_HALO, :] = jnp.where(t > 0, halo_ref[...], 0.0)
    buf_ref[POOL_HALO:, :] = a
    pos = t * tt + lax.broadcasted_iota(jnp.int32, (tt, 1), 0)
    for gi, w in enumerate(POOL_WINDOWS):
        cols = slice(gi * gch, (gi + 1) * gch)
        s = a[:, cols]
        for k in range(1, w):
            s = s + buf_ref[POOL_HALO - k:POOL_HALO - k + tt, cols]
        cnt = jnp.minimum(w, pos + 1).astype(F32)
        pooled = s / cnt - a[:, cols]
        mixed = _dot(pooled.astype(BF16), pw_ref[gi])
        oa_ref[:, cols] = (mixed * ps_ref[:, cols]).astype(BF16)

    u = _gelu(u_ref[...])
    vn = _layernorm_rows(_gelu(v_ref[...]), lg_ref[...], lb_ref[...])
    hch = ch // SGU_HEADS
    row = lax.broadcasted_iota(jnp.int32, (SGU_CHUNK, SGU_CHUNK), 0)
    col = lax.broadcasted_iota(jnp.int32, (SGU_CHUNK, SGU_CHUNK), 1)
    for h in range(SGU_HEADS):
        cols = slice(h * hch, (h + 1) * hch)
        wm = jnp.where(row >= col, ws_ref[h], 0.0).astype(BF16)
        bias = bst_ref[:, h:h + 1]
        for c in range(tt // SGU_CHUNK):
            rows = slice(c * SGU_CHUNK, (c + 1) * SGU_CHUNK)
            mixed = _dot(wm, vn[rows, cols].astype(BF16)) + bias
            ob_ref[rows, cols] = (u[rows, cols] * mixed).astype(BF16)


def _even_prompt(z3, lyr, pool_w, pool_scale, ln_g, ln_b, ws, bst, tt):
    b, t_len, _ = z3.shape
    ch = pool_scale.shape[-1]
    hb = tt // POOL_HALO
    out = jax.ShapeDtypeStruct((b, t_len, ch), BF16)
    return pl.pallas_call(
        _even_prompt_kernel,
        grid=(b, t_len // tt),
        in_specs=[
            pl.BlockSpec((None, tt, ch), lambda i, t: (i, t, 0)),
            pl.BlockSpec((None, POOL_HALO, ch), lambda i, t: (i, jnp.maximum(t * hb - 1, 0), 0)),
            pl.BlockSpec((None, tt, ch), lambda i, t: (i, t, 1)),
            pl.BlockSpec((None, tt, ch), lambda i, t: (i, t, 2)),
            _whole(lyr, pool_w), _whole(lyr, pool_scale), _whole(lyr, ln_g), _whole(lyr, ln_b),
            _whole(lyr, ws), _whole(lyr, bst),
        ],
        out_specs=[pl.BlockSpec((None, tt, ch), lambda i, t: (i, t, 0))] * 2,
        out_shape=[out, out],
        scratch_shapes=[pltpu.VMEM((POOL_HALO + tt, ch), F32)],
        compiler_params=_params("parallel", "arbitrary"),
        name="even_prompt",
    )(z3, z3, z3, z3, pool_w, pool_scale, ln_g, ln_b, ws, bst)


def _even_sample_kernel(a_ref, u_ref, v_ref, st_ref, pw_ref, ps_ref, lg_ref, lb_ref, w0_ref, b0_ref,
                        oa_ref, ob_ref, nst_ref, nv_ref):
    ch = a_ref.shape[1]
    gch = ch // len(POOL_WINDOWS)
    a = a_ref[...]
    for gi, w in enumerate(POOL_WINDOWS):
        cols = slice(gi * gch, (gi + 1) * gch)
        s = a[:, cols]
        for k in range(1, w):
            s = s + st_ref[POOL_HIST - k, :, cols]
        pooled = s / float(min(w, PAST_LEN + 1)) - a[:, cols]
        mixed = _dot(pooled.astype(BF16), pw_ref[gi])
        oa_ref[:, cols] = (mixed * ps_ref[:, cols]).astype(BF16)
    nst_ref[0:POOL_HIST - 1] = st_ref[1:POOL_HIST]
    nst_ref[POOL_HIST - 1] = a

    u = _gelu(u_ref[...])
    vn = _layernorm_rows(_gelu(v_ref[...]), lg_ref[...], lb_ref[...])
    nv_ref[...] = vn
    ob_ref[...] = (u * (vn * w0_ref[...] + b0_ref[...])).astype(BF16)


def _even_sample(z, lyr, state, pool_w, pool_scale, ln_g, ln_b, w0, b0, carried, bt):
    assert PAST_LEN % SGU_CHUNK == 0
    n = z.shape[0]
    ch = pool_scale.shape[-1]
    act = jax.ShapeDtypeStruct((n, ch), BF16)
    rows = pl.BlockSpec((bt, ch), lambda i: (i, 0))
    n_in = 10
    stacked = {2: ((POOL_HIST, bt, ch), lambda i: (0, i, 0)), 3: ((bt, ch), lambda i: (i, 0))}
    call, extra = _stacked_call(
        _even_sample_kernel, n_in, carried, stacked, lyr,
        grid=(n // bt,),
        in_specs=[
            rows, pl.BlockSpec((bt, ch), lambda i: (i, 1)), pl.BlockSpec((bt, ch), lambda i: (i, 2)),
            _layer(lyr, (POOL_HIST, bt, ch), lambda i: (0, i, 0)),
            _whole(lyr, pool_w), _whole(lyr, pool_scale), _whole(lyr, ln_g), _whole(lyr, ln_b),
            _whole(lyr, w0), _whole(lyr, b0),
        ],
        out_specs=[rows, rows, None, None],
        out_shape=[act, act, jax.ShapeDtypeStruct(state.shape, F32),
                   jax.ShapeDtypeStruct((state.shape[0], n, ch), F32)],
        compiler_params=_params("parallel"),
        name="even_sample",
    )
    return call(z, z, z, state, pool_w, pool_scale, ln_g, ln_b, w0, b0, *extra)


def _conv_prompt_kernel(p_ref, q_ref, hp_ref, hq_ref, dw_ref, db_ref, lg_ref, lb_ref, o_ref, st_ref, buf_ref,
                        rot_ref, y_ref, wb_ref):
    t = pl.program_id(1)
    tt = p_ref.shape[0]
    hist = CONV_WIDTH - 1
    glu = p_ref[...] * jax.nn.sigmoid(q_ref[...])
    halo = hp_ref[...] * jax.nn.sigmoid(hq_ref[...])
    buf_ref[0:CONV_HALO, :] = jnp.where(t > 0, halo, 0.0)
    buf_ref[CONV_HALO:, :] = glu
    span = rot_ref.shape[1]
    for r in range(1, SUBLANES):
        rot_ref[r - 1] = buf_ref[r:r + span, :]
    base = CONV_HALO - hist
    for k in range(CONV_WIDTH):
        wb_ref[k] = jnp.broadcast_to(dw_ref[k:k + 1, :], wb_ref.shape[1:])
    subs = range(CONV_ROWS // SUBLANES)
    lane_chunks = p_ref.shape[1] // CONV_LANES

    def chunk(i, carry):
        r0 = pl.multiple_of((i // lane_chunks) * CONV_ROWS, CONV_ROWS)
        lanes = pl.ds(pl.multiple_of((i % lane_chunks) * CONV_LANES, CONV_LANES), CONV_LANES)
        at = lambda src, off, j: src[pl.ds(off + r0 + j * SUBLANES, SUBLANES), lanes]
        bias = db_ref[:, lanes]
        w = wb_ref[hist, :, lanes]
        acc = [bias + w * at(buf_ref, CONV_HALO, j) for j in subs]
        for k in range(hist):
            r = (base + k) % SUBLANES
            a = base + k - r
            src = buf_ref if r == 0 else rot_ref.at[r - 1]
            w = wb_ref[k, :, lanes]
            acc = [acc[j] + w * at(src, a, j) for j in subs]
        for j in subs:
            y_ref[pl.ds(r0 + j * SUBLANES, SUBLANES), lanes] = acc[j]
        return carry

    lax.fori_loop(0, (tt // CONV_ROWS) * lane_chunks, chunk, 0, unroll=2)
    o_ref[...] = _silu(_layernorm_rows(y_ref[...], lg_ref[...], lb_ref[...])).astype(BF16)

    @pl.when(t == pl.num_programs(1) - 1)
    def _():
        st_ref[...] = buf_ref[CONV_HALO + tt - hist:CONV_HALO + tt, :]


def _conv_prompt(z3, lyr, n_lyr, dw, db, ln_g, ln_b, carried, tt):
    b, t_len, _ = z3.shape
    ch = db.shape[-1]
    hb = tt // CONV_HALO
    hist = CONV_WIDTH - 1
    halo = lambda c: pl.BlockSpec((None, CONV_HALO, ch), lambda i, t: (i, jnp.maximum(t * hb - 1, 0), c))
    stacked = {1: ((None, hist, ch), lambda i, t: (i, 0, 0))}
    call, extra = _stacked_call(
        _conv_prompt_kernel, 8, carried, stacked, lyr,
        grid=(b, t_len // tt),
        in_specs=[
            pl.BlockSpec((None, tt, ch), lambda i, t: (i, t, 0)),
            pl.BlockSpec((None, tt, ch), lambda i, t: (i, t, 1)),
            halo(0), halo(1),
            _whole(lyr, dw), _whole(lyr, db), _whole(lyr, ln_g), _whole(lyr, ln_b),
        ],
        out_specs=[
            pl.BlockSpec((None, tt, ch), lambda i, t: (i, t, 0)),
            None,
        ],
        out_shape=[jax.ShapeDtypeStruct((b, t_len, ch), BF16), jax.ShapeDtypeStruct((n_lyr, b, hist, ch), F32)],
        scratch_shapes=[pltpu.VMEM((CONV_HALO + tt, ch), F32),
                        pltpu.VMEM((SUBLANES - 1, CONV_HALO + tt - SUBLANES, ch), F32),
                        pltpu.VMEM((tt, ch), F32), pltpu.VMEM((CONV_WIDTH, SUBLANES, ch), F32)],
        compiler_params=_params("parallel", "arbitrary"),
        name="conv_prompt",
    )
    return call(z3, z3, z3, z3, dw, db, ln_g, ln_b, *extra)


def _l2norm_heads(x, scale):
    outs = []
    for h in range(x.shape[1] // DN_DK):
        xh = x[:, h * DN_DK:(h + 1) * DN_DK]
        outs.append(xh * (lax.rsqrt(jnp.sum(xh * xh, -1, keepdims=True) + EPS) * scale))
    return jnp.concatenate(outs, -1)


def _dn_short_conv(raw_refs, rows, w_ref, first, tail_ref, buf_ref, out_ref):
    hist = DN_CONV - 1
    base = DN_HALO - hist
    tt = rows.stop - rows.start
    for part, (x_ref, scale) in enumerate(zip(raw_refs, (DN_DK ** -0.5, 1.0, None))):
        ch = x_ref.shape[1]
        cols = slice(part * ch, (part + 1) * ch)
        x = x_ref[rows, :]
        buf_ref[0:DN_HALO, :] = jnp.where(first, 0.0, tail_ref[part])
        buf_ref[DN_HALO:, :] = x
        tail_ref[part] = x[tt - DN_HALO:, :]
        acc = w_ref[hist:hist + 1, cols] * x
        for k in range(hist):
            acc = acc + w_ref[k:k + 1, cols] * buf_ref[base + k:base + k + tt, :]
        y = _silu(acc)
        out_ref[part, rows, :] = y if scale is None else _l2norm_heads(y, scale)


def _gate_terms(ab, alog, dtb):
    g = -jnp.exp(alog) * jax.nn.softplus(ab + dtb)
    return g, jax.nn.sigmoid(ab)


def _each(fn, *lists):
    return [fn(*xs) for xs in zip(*lists)]


def _delta_chunk_terms(q, k, v, gi, gj, bi, masks):
    lower, strict, same_block = masks
    n = q[0].shape[0]
    decay = _each(lambda a, b: jnp.where(lower, jnp.exp(jnp.where(lower, a - b, 0.0)), 0.0), gi, gj)
    eg = _each(jnp.exp, gi)
    kb = _each(jnp.multiply, k, bi)
    kk = _each(_bdot_nt, kb, k)
    amat = _each(lambda a, d: jnp.where(strict, a * d, 0.0), kk, decay)
    ad = _each(lambda a: jnp.where(same_block, a, 0.0), amat)
    an = _each(jnp.subtract, amat, ad)
    doff = _each(jnp.negative, ad)
    pw = ad
    for _ in range(DN_SOLVE_BLOCK.bit_length() - 2):
        pw = _each(_bdot, pw, pw)
        dp = _each(_bdot, doff, pw)
        doff = _each(lambda d, p, x: d + p + x, doff, pw, dp)
    assert n // DN_SOLVE_BLOCK == 4
    bm = _each(jnp.add, an, _each(_bdot, doff, an))
    rhs = _each(lambda vv, b, kk_, e: jnp.concatenate([vv * b, kk_ * e], -1), v, bi, kb, eg)
    sol = _each(jnp.add, rhs, _each(_bdot, doff, rhs))
    b2 = _each(_bdot, bm, bm)
    sol = _each(jnp.add, sol, _each(_bdot, b2, sol))
    sol = _each(jnp.subtract, sol, _each(_bdot, bm, sol))
    qk = _each(jnp.multiply, _each(_bdot_nt, q, k), decay)
    g_last = _each(lambda a: a[n - 1:n, :], gi)
    return (_each(lambda x: x[:, :DN_DK], sol), _each(lambda x: x[:, DN_DK:], sol), qk, _each(jnp.multiply, q, eg),
            _each(lambda kk_, gl, a: kk_ * jnp.exp(gl - a), k, g_last, gi), _each(jnp.exp, g_last))


def _delta_prompt_kernel(q_raw, k_raw, v_raw, zg_ref, ab_ref, cw_ref, alog_ref, dtb_ref, ng_ref, o_ref, sfin_ref,
                         s_ref, tail_ref, buf_ref, qkv_ref):
    c = pl.program_id(1)
    n = DN_CHUNK
    heads = range(DN_HEADS)
    chunks = range(q_raw.shape[0] // n)

    @pl.when(c == 0)
    def _():
        s_ref[...] = jnp.zeros_like(s_ref)

    groups = [chunks[g:g + DN_PREP_CHUNKS] for g in range(0, len(chunks), DN_PREP_CHUNKS)]
    for g, group in enumerate(groups):
        span = slice(group[0] * n, (group[-1] + 1) * n)
        _dn_short_conv((q_raw, k_raw, v_raw), span, cw_ref, (c == 0) if g == 0 else False, tail_ref, buf_ref,
                       qkv_ref)
    q_ref, k_ref, v_ref = qkv_ref.at[0], qkv_ref.at[1], qkv_ref.at[2]

    row = lax.broadcasted_iota(jnp.int32, (n, n), 0)
    col = lax.broadcasted_iota(jnp.int32, (n, n), 1)
    lower = row >= col
    masks = (lower, row > col, (row // DN_SOLVE_BLOCK) == (col // DN_SOLVE_BLOCK))
    tri = lower.astype(F32)
    rows = [slice(j * n, (j + 1) * n) for j in chunks]
    cols = [slice(h * DN_DK, (h + 1) * DN_DK) for h in heads]
    gate = [_gate_terms(ab_ref[r, :], alog_ref[...], dtb_ref[...]) for r in rows]
    gcol = [_dot_hi(tri, g) for g, _ in gate]
    grow = [_dot_nt_hi(g.T, tri) for g, _ in gate]
    terms = []
    for group in groups:
        units = [(j, h) for j in group for h in heads]
        terms.append(_delta_chunk_terms(
            [q_ref[rows[j], cols[h]] for j, h in units], [k_ref[rows[j], cols[h]] for j, h in units],
            [v_ref[rows[j], cols[h]] for j, h in units], [gcol[j][:, h:h + 1] for j, h in units],
            [grow[j][h:h + 1, :] for j, h in units],
            [gate[j][1][:, DN_HEADS + h:DN_HEADS + h + 1] for j, h in units], masks))
    u, w, qk, qd, kd, gl = (sum((list(t[i]) for t in terms), []) for i in range(6))
    s = [s_ref[h] for h in heads]
    for j in chunks:
        at = lambda xs: xs[j * DN_HEADS:(j + 1) * DN_HEADS]
        v_new = _each(jnp.subtract, at(u), _each(_bdot, at(w), s))
        o = _each(jnp.add, _each(_bdot, at(qd), s), _each(_bdot, at(qk), v_new))
        s = _each(lambda ss, g, x: ss * g + x, s, at(gl), _each(_bdot_tn, at(kd), v_new))
        for h in heads:
            on = o[h] * lax.rsqrt(jnp.mean(o[h] * o[h], -1, keepdims=True) + EPS) * ng_ref[...]
            o_ref[rows[j], cols[h]] = (on * _silu(zg_ref[rows[j], cols[h]])).astype(BF16)
    for h in heads:
        s_ref[h] = s[h]

    @pl.when(c == pl.num_programs(1) - 1)
    def _():
        sfin_ref[...] = s_ref[...]


def _delta_prompt(z3, qkv_col, zg_col, ab3, lyr, n_lyr, conv_w, a_log, dt_bias, norm_g, carried):
    b, t_len, _ = z3.shape
    ch = DN_HEADS * DN_DK
    n = DN_CHUNK * DN_STEP_CHUNKS
    zcol = lambda col: pl.BlockSpec((None, n, ch), lambda i, c: (i, c, col))
    stacked = {1: ((None, DN_HEADS, DN_DK, DN_DK), lambda i, c: (i, 0, 0, 0))}
    call, extra = _stacked_call(
        _delta_prompt_kernel, 9, carried, stacked, lyr,
        grid=(b, t_len // n),
        in_specs=[
            zcol(qkv_col), zcol(qkv_col + 1), zcol(qkv_col + 2), zcol(zg_col),
            pl.BlockSpec((None, n, ab3.shape[-1]), lambda i, c: (i, c, 0)),
            _whole(lyr, conv_w), _whole(lyr, a_log), _whole(lyr, dt_bias), _whole(lyr, norm_g),
        ],
        out_specs=[
            pl.BlockSpec((None, n, ch), lambda i, c: (i, c, 0)),
            None,
        ],
        out_shape=[jax.ShapeDtypeStruct((b, t_len, ch), BF16),
                   jax.ShapeDtypeStruct((n_lyr, b, DN_HEADS, DN_DK, DN_DK), F32)],
        scratch_shapes=[pltpu.VMEM((DN_HEADS, DN_DK, DN_DK), F32), pltpu.VMEM((3, DN_HALO, ch), F32),
                        pltpu.VMEM((DN_HALO + DN_CHUNK * DN_PREP_CHUNKS, ch), F32), pltpu.VMEM((3, n, ch), F32)],
        compiler_params=_params("parallel", "arbitrary"),
        name="delta_prompt",
    )
    return call(z3, z3, z3, z3, ab3, conv_w, a_log, dt_bias, norm_g, *extra)


def _odd_sample_kernel(p_ref, gt_ref, q_ref, k_ref, v_ref, zg_ref, ab_ref, cst_ref, dst_ref, s_ref,
                       dw_ref, db_ref, lg_ref, lb_ref, cw_ref, alog_ref, dtb_ref, ng_ref,
                       oc_ref, od_ref, ncst_ref, ndst_ref, ns_ref, obuf_ref):
    bt, ch = p_ref.shape
    hist = CONV_WIDTH - 1
    glu = p_ref[...] * jax.nn.sigmoid(gt_ref[...])
    acc = db_ref[...] + dw_ref[hist:hist + 1, :] * glu
    for k in range(hist):
        acc = acc + dw_ref[k:k + 1, :] * cst_ref[k]
    oc_ref[...] = _silu(_layernorm_rows(acc, lg_ref[...], lb_ref[...])).astype(BF16)
    ncst_ref[0:hist - 1] = cst_ref[1:hist]
    ncst_ref[hist - 1] = glu

    dh = DN_CONV - 1
    parts = []
    for pi, x_ref in enumerate((q_ref, k_ref, v_ref)):
        cols = slice(pi * ch, (pi + 1) * ch)
        x = x_ref[...]
        y = cw_ref[dh:dh + 1, cols] * x
        for k in range(dh):
            y = y + cw_ref[k:k + 1, cols] * dst_ref[k, :, cols]
        parts.append(_silu(y))
        ndst_ref[0:dh - 1, :, cols] = dst_ref[1:dh, :, cols]
        ndst_ref[dh - 1, :, cols] = x
    q = _l2norm_heads(parts[0], DN_DK ** -0.5)
    k = _l2norm_heads(parts[1], 1.0)
    v = parts[2]
    g, beta = _gate_terms(ab_ref[...], alog_ref[...], dtb_ref[...])
    eg_all = jnp.exp(g)

    assert 2 * DN_HEADS * bt == LANES
    kq = jnp.concatenate([x[:, h * DN_DK:(h + 1) * DN_DK] for x in (k, q) for h in range(DN_HEADS)], 0)
    kq_t = kq.T
    for h in range(DN_HEADS):
        cols = slice(h * DN_DK, (h + 1) * DN_DK)
        for b in range(bt):
            kc = kq_t[:, h * bt + b:h * bt + b + 1]
            qc = kq_t[:, (DN_HEADS + h) * bt + b:(DN_HEADS + h) * bt + b + 1]
            s = s_ref[b, h]
            sk = jnp.sum(s * kc, 0, keepdims=True)
            sq = jnp.sum(s * qc, 0, keepdims=True)
            eg = eg_all[b:b + 1, h:h + 1]
            bi = beta[b:b + 1, DN_HEADS + h:DN_HEADS + h + 1]
            v_new = bi * (v[b:b + 1, cols] - eg * sk)
            qk = jnp.sum(q[b:b + 1, cols] * k[b:b + 1, cols], -1, keepdims=True)
            obuf_ref[b:b + 1, cols] = eg * sq + qk * v_new
            ns_ref[b, h] = s * eg + kc * v_new
    for h in range(DN_HEADS):
        cols = slice(h * DN_DK, (h + 1) * DN_DK)
        o = obuf_ref[:, cols]
        o = o * lax.rsqrt(jnp.mean(o * o, -1, keepdims=True) + EPS) * ng_ref[...]
        od_ref[:, cols] = (o * _silu(zg_ref[:, cols])).astype(BF16)


def _odd_sample(z, ab, lyr, cstate, dstate, sstate, dw, db, ln_g, ln_b, conv_w, a_log, dt_bias, norm_g, carried, bt):
    n = z.shape[0]
    ch = db.shape[-1]
    zcol = lambda c: pl.BlockSpec((bt, ch), lambda i: (i, c))
    lead_blk = lambda a: ((bt,) + a.shape[2:], lambda i: (i,) + (0,) * (a.ndim - 2))
    hist_blk = lambda a: ((a.shape[1], bt, a.shape[3]), lambda i: (0, i, 0))
    lead = lambda a: _layer(lyr, *lead_blk(a))
    hist = lambda a: _layer(lyr, *hist_blk(a))
    act = jax.ShapeDtypeStruct((n, ch), BF16)
    same = lambda a: jax.ShapeDtypeStruct(a.shape, F32)
    stacked = {2: hist_blk(cstate), 3: hist_blk(dstate), 4: lead_blk(sstate)}
    call, extra = _stacked_call(
        _odd_sample_kernel, 18, carried, stacked, lyr,
        grid=(n // bt,),
        in_specs=[zcol(0), zcol(1), zcol(2), zcol(3), zcol(4), zcol(5),
                  pl.BlockSpec((bt, ab.shape[-1]), lambda i: (i, 0)),
                  hist(cstate), hist(dstate), lead(sstate),
                  _whole(lyr, dw), _whole(lyr, db), _whole(lyr, ln_g), _whole(lyr, ln_b),
                  _whole(lyr, conv_w), _whole(lyr, a_log), _whole(lyr, dt_bias), _whole(lyr, norm_g)],
        out_specs=[zcol(0), zcol(0), None, None, None],
        out_shape=[act, act, same(cstate), same(dstate), same(sstate)],
        scratch_shapes=[pltpu.VMEM((bt, ch), F32)],
        compiler_params=_params("parallel"),
        name="odd_sample",
    )
    return call(z, z, z, z, z, z, ab, cstate, dstate, sstate, dw, db, ln_g, ln_b, conv_w, a_log, dt_bias, norm_g,
                *extra)


def kernel(x_prompt, x_sample, state_pool, state_conv_c, state_dn_conv, state_dn_S, norm_mix, norm_ffn, norm_final, ev_w_in, pool_w, pool_scale, sgu_ln_g, sgu_ln_b, sgu_ws, sgu_b, ev_w_out, od_w_in, cv_dw, cv_db, cv_ln_g, cv_ln_b, dn_conv_w, dn_a_log, dn_dt_bias, dn_norm_g, od_w_out, ffn_w_up, ffn_w_down):
    bp, t_len, d = x_prompt.shape
    bs = x_sample.shape[0]
    assert x_sample.shape[1] == 1
    depth = norm_mix.shape[0]
    n_even, n_odd = ev_w_in.shape[0], od_w_in.shape[0]
    ch = pool_scale.shape[1]
    odd_main = 6 * ch
    n_gate = 2 * DN_HEADS
    assert od_w_in.shape[2] == odd_main + n_gate and odd_main % n_gate == 0

    rows = lambda a: a.reshape(a.shape[0], 1, a.shape[-1])
    gate_rows = lambda a: rows(jnp.pad(a, ((0, 0), (0, DN_HEADS))))
    od_in = jnp.swapaxes(od_w_in, 1, 2)
    pool_wb = pool_w.astype(BF16)
    g_mix, g_ffn, g_fin = rows(norm_mix), rows(norm_ffn), norm_final.reshape(1, d)
    p_scale, s_lg, s_lb = rows(pool_scale), rows(sgu_ln_g), rows(sgu_ln_b)
    s_bt = jnp.swapaxes(sgu_b, 1, 2)
    s_w0 = rows(jnp.repeat(sgu_ws[:, :, 0, 0], ch // SGU_HEADS, axis=1))
    s_b0 = rows(jnp.repeat(sgu_b[:, :, 0], ch // SGU_HEADS, axis=1))
    c_db, c_lg, c_lb = rows(cv_db), rows(cv_ln_g), rows(cv_ln_b)
    a_log, dt_bias, d_ng = gate_rows(dn_a_log), gate_rows(dn_dt_bias), rows(dn_norm_g)

    hist_major = lambda a: jnp.swapaxes(a, 1, 2)
    st_pool, st_cc, st_dc = hist_major(state_pool), hist_major(state_conv_c), hist_major(state_dn_conv)

    xp = x_prompt.reshape(bp * t_len, d)
    xs = x_sample.reshape(bs, d)
    tn = _tiles(bp * t_len)[2]
    flat = lambda a: a.reshape(bp * t_len, a.shape[-1])
    pool_p, dc_p = [], []
    even_s = cc_p = s_p = odd_s = None
    for l in range(depth):
        i = l // 2
        if l % 2 == 0:
            zs, w_in = _norm_matmul(xs, g_mix, l, ev_w_in, i, 3 * ch, tn, emit_weights=True)
            zp, = _norm_matmul(xp, g_mix, l, w_in, 0, 3 * ch, tn)
            zp3 = zp.reshape(bp, t_len, 3 * ch)
            a_p, b_p = _even_prompt(zp3, i, pool_wb, p_scale, s_lg, s_lb, sgu_ws, s_bt, 256)
            a_s, b_s, *even_s = _even_sample(zs, i, st_pool, pool_wb, p_scale, s_lg, s_lb, s_w0, s_b0, even_s, 8)
            pool_p.append(zp3[:, t_len - POOL_HIST:, :ch])
            xs, w_out = _mix_out(a_s, b_s, ev_w_out, i, xs, emit_weights=True)
            xp = _mix_out(flat(a_p), flat(b_p), w_out, 0, xp)
        else:
            zp, abp = _norm_matmul(xp, g_mix, l, od_in, i, odd_main, tn, True, n_gate)
            zs, abs_ = _norm_matmul(xs, g_mix, l, od_in, i, odd_main, tn, True, n_gate)
            zp3 = zp.reshape(bp, t_len, odd_main)
            c_p, *cc_p = _conv_prompt(zp3, i, n_odd, cv_dw, c_db, c_lg, c_lb, cc_p, 256)
            d_p, *s_p = _delta_prompt(zp3, 2, 5, abp.reshape(bp, t_len, n_gate), i, n_odd, dn_conv_w, a_log, dt_bias,
                                      d_ng, s_p)
            c_s, d_s, *odd_s = _odd_sample(zs, abs_, i, st_cc, st_dc, state_dn_S, cv_dw, c_db, c_lg,
                                           c_lb, dn_conv_w, a_log, dt_bias, d_ng, odd_s, 8)
            dc_p.append(zp3[:, t_len - (DN_CONV - 1):, 2 * ch:5 * ch])
            xs, w_out = _mix_out(c_s, d_s, od_w_out, i, xs, emit_weights=True)
            xp = _mix_out(flat(c_p), flat(d_p), w_out, 0, xp)
        last = l == depth - 1
        xs, *w_ffn = _ffn(xs, g_ffn, l, (ffn_w_up, ffn_w_down), g_fin, last)
        xp, = _ffn(xp, g_ffn, l, w_ffn, g_fin, last)
    pool_s, v_s = even_s
    cc_s, dc_s, s_s = odd_s
    return (xp.reshape(bp, t_len, d), xs.reshape(bs, 1, d), jnp.stack(pool_p), hist_major(pool_s),
            v_s.reshape(n_even, bs, 1, ch), cc_p[0], hist_major(cc_s), jnp.stack(dc_p), hist_major(dc_s), s_p[0], s_s)
```

```python
import functools

import jax
import jax.numpy as jnp
from jax import lax
from jax.experimental import pallas as pl
from jax.experimental.pallas import tpu as pltpu

F32 = jnp.float32
BF16 = jnp.bfloat16
HI = lax.Precision.HIGHEST

EPS = 1e-6
PAST_LEN = 16384
POOL_WINDOWS = (2, 4, 8, 16)
POOL_HIST = max(POOL_WINDOWS) - 1
SGU_HEADS = 4
SGU_CHUNK = 128
CONV_WIDTH = 31
DN_HEADS = 8
DN_DK = 128
DN_CONV = 4
DN_CHUNK = 64
DN_SOLVE_BLOCK = 16
DN_STEP_CHUNKS = 4
DN_PREP_CHUNKS = 2

LANES = 128
SUBLANES = 8
VMEM_LIMIT = 56 * 1024 * 1024

POOL_HALO = 16
CONV_HALO = 32
DN_HALO = 8
CONV_ROWS, CONV_LANES = 64, 256


def _tiles(m):
    return min(m, 1024), min(m, 512), 512


def _params(*semantics):
    return pltpu.CompilerParams(dimension_semantics=semantics, vmem_limit_bytes=VMEM_LIMIT)


def _layer(lyr, block, index_map):
    return pl.BlockSpec((None,) + tuple(block), lambda *g: (lyr,) + tuple(index_map(*g)))


def _whole(lyr, arr):
    return pl.BlockSpec((None,) + arr.shape[1:], lambda *g: (lyr,) + (0,) * (arr.ndim - 1))


def _skip_refs(fn, start, count):
    def body(*refs):
        return fn(*refs[:start], *refs[start + count:])
    return body


def _own_layer_zero_rest(fn, n_in, outs, lyr):
    def body(*refs):
        refs = list(refs)
        for o in outs:
            full = refs[n_in + o]
            for other in range(full.shape[0]):
                if other != lyr:
                    full[other] = jnp.zeros(full.shape[1:], full.dtype)
            refs[n_in + o] = full.at[lyr]
        return fn(*refs)
    return body


def _stacked_call(kernel_fn, n_in, carried, stacked_out, lyr, **kw):
    out_specs = list(kw.pop("out_specs"))
    if carried is None:
        for o, (block, imap) in stacked_out.items():
            n_lyr = kw["out_shape"][o].shape[0]
            out_specs[o] = pl.BlockSpec((n_lyr,) + tuple(block), lambda *g, imap=imap: (0,) + tuple(imap(*g)))
        body = _own_layer_zero_rest(kernel_fn, n_in, tuple(stacked_out), lyr)
        return pl.pallas_call(body, out_specs=out_specs, **kw), ()
    for o, (block, imap) in stacked_out.items():
        out_specs[o] = _layer(lyr, block, imap)
    kw["in_specs"] = list(kw["in_specs"]) + [pl.BlockSpec(memory_space=pl.ANY)] * len(carried)
    aliases = {n_in + k: o for k, o in enumerate(stacked_out)}
    body = _skip_refs(kernel_fn, n_in, len(carried))
    return pl.pallas_call(body, out_specs=out_specs, input_output_aliases=aliases, **kw), tuple(carried)


def _rms_rows(x, g):
    return x * lax.rsqrt(jnp.mean(x * x, -1, keepdims=True) + EPS) * g


def _layernorm_rows(x, g, b):
    mu = jnp.mean(x, -1, keepdims=True)
    xc = x - mu
    var = jnp.mean(xc * xc, -1, keepdims=True)
    return xc * lax.rsqrt(var + EPS) * g + b


def _gelu(x):
    return 0.5 * x * (1.0 + lax.erf(x * 0.7071067811865476))


def _silu(x):
    return x * jax.nn.sigmoid(x)


def _dot(a, b):
    return jnp.dot(a, b, preferred_element_type=F32)


def _dot_hi(a, b):
    return jnp.dot(a, b, precision=HI, preferred_element_type=F32)


def _dot_nt_hi(a, b):
    return lax.dot_general(a, b, (((1,), (1,)), ((), ())), precision=HI, preferred_element_type=F32)


def _bdot(a, b):
    return jnp.dot(a.astype(BF16), b.astype(BF16), preferred_element_type=F32)


def _bdot_nt(a, b):
    return lax.dot_general(a.astype(BF16), b.astype(BF16), (((1,), (1,)), ((), ())), preferred_element_type=F32)


def _bdot_tn(a, b):
    return lax.dot_general(a.astype(BF16), b.astype(BF16), (((0,), (0,)), ((), ())), preferred_element_type=F32)


def _dot_nt(a, b):
    return lax.dot_general(a, b, (((1,), (1,)), ((), ())), preferred_element_type=F32)


def _norm_matmul_kernel(x_ref, g_ref, w_ref, *rest, w_rows_out, n_tail, emit_weights):
    o_ref, h_ref = (rest[0], rest[-1]) if not n_tail else (rest[1], rest[3])

    @pl.when(pl.program_id(1) == 0)
    def _():
        h = _rms_rows(x_ref[...], g_ref[...]).astype(BF16)
        h_ref[...] = h
        if n_tail:
            wt_ref, tail_ref = rest[0], rest[2]
            tail_ref[...] = _dot_nt(h, wt_ref[...].astype(BF16))

    w = w_ref[...].astype(BF16)
    if emit_weights:
        rest[1][...] = w
    o_ref[...] = _dot_nt(h_ref[...], w) if w_rows_out else _dot(h_ref[...], w)


def _norm_matmul(x, gains, lyr, w, wl, n_out, tn, w_rows_out=False, n_tail=0, emit_weights=False):
    m, d = x.shape
    tm = _tiles(m)[0]
    if w_rows_out:
        w_spec = _layer(wl, (tn, d), lambda i, j: (j, 0))
    else:
        assert not n_tail
        w_spec = _layer(wl, (d, tn), lambda i, j: (0, j))
    in_specs = [pl.BlockSpec((tm, d), lambda i, j: (i, 0)), _whole(lyr, gains), w_spec]
    out_specs = [pl.BlockSpec((tm, tn), lambda i, j: (i, j))]
    out_shape = [jax.ShapeDtypeStruct((m, n_out), F32)]
    operands = [x, gains, w]
    if n_tail:
        assert n_out % n_tail == 0
        in_specs.append(_layer(wl, (n_tail, d), lambda i, j: (n_out // n_tail, 0)))
        out_specs.append(pl.BlockSpec((tm, n_tail), lambda i, j: (i, 0)))
        out_shape.append(jax.ShapeDtypeStruct((m, n_tail), F32))
        operands.append(w)
    if emit_weights:
        assert m == tm and not w_rows_out and not n_tail
        out_specs.append(pl.BlockSpec((None, d, tn), lambda i, j: (0, 0, j)))
        out_shape.append(jax.ShapeDtypeStruct((1, d, n_out), BF16))
    return pl.pallas_call(
        functools.partial(_norm_matmul_kernel, w_rows_out=w_rows_out, n_tail=n_tail, emit_weights=emit_weights),
        grid=(m // tm, n_out // tn),
        in_specs=in_specs,
        out_specs=out_specs,
        out_shape=out_shape,
        scratch_shapes=[pltpu.VMEM((tm, d), BF16)],
        compiler_params=_params("parallel", "arbitrary"),
        name="norm_matmul",
    )(*operands)


def _mix_out_kernel(a0_ref, a1_ref, w0_ref, w1_ref, x_ref, o_ref, *w_out):
    w0, w1 = w0_ref[...].astype(BF16), w1_ref[...].astype(BF16)
    if w_out:
        w_out[0][0], w_out[0][1] = w0, w1
    o_ref[...] = x_ref[...] + _dot(a0_ref[...], w0) + _dot(a1_ref[...], w1)


def _mix_out(a0, a1, w, wl, x, emit_weights=False):
    m, kh = a0.shape
    n = w.shape[2]
    tm = _tiles(m)[1]
    assert not emit_weights or m == tm
    w_half = lambda r: pl.BlockSpec((None, kh, n), lambda i: (wl, r, 0), pipeline_mode=pl.Buffered(1))
    rows = pl.BlockSpec((tm, n), lambda i: (i, 0))
    out = pl.pallas_call(
        _mix_out_kernel,
        grid=(m // tm,),
        in_specs=[
            pl.BlockSpec((tm, kh), lambda i: (i, 0)),
            pl.BlockSpec((tm, kh), lambda i: (i, 0)),
            w_half(0), w_half(1),
            rows,
        ],
        out_specs=[rows] + [pl.BlockSpec((2, kh, n), lambda i: (0, 0, 0))] * emit_weights,
        out_shape=[jax.ShapeDtypeStruct((m, n), F32)] + [jax.ShapeDtypeStruct((2, kh, n), BF16)] * emit_weights,
        compiler_params=_params("parallel"),
        name="mix_out",
    )(a0, a1, w, w, x)
    return (out[0], out[1].reshape(1, 2 * kh, n)) if emit_weights else out[0]


def _ffn_kernel(x_ref, g_ref, wg_ref, wu_ref, wd_ref, gf_ref, o_ref, *rest, final_norm, emit_weights):
    h_ref = rest[-1]
    f = pl.program_id(1)

    @pl.when(f == 0)
    def _():
        x = x_ref[...]
        h_ref[...] = _rms_rows(x, g_ref[...]).astype(BF16)
        o_ref[...] = x

    wg, wu, wd = wg_ref[...].astype(BF16), wu_ref[...].astype(BF16), wd_ref[...].astype(BF16)
    if emit_weights:
        wg_out, wu_out, wd_out = rest[:3]
        wg_out[...] = wg
        wu_out[...] = wu
        wd_out[...] = wd
    h = h_ref[...]
    act = _silu(_dot(h, wg)) * _dot(h, wu)
    o_ref[...] += _dot(act.astype(BF16), wd)

    if final_norm:
        @pl.when(f == pl.num_programs(1) - 1)
        def _():
            o_ref[...] = _rms_rows(o_ref[...], gf_ref[...])


def _ffn(x, gains, lyr, weights, g_final, final_norm):
    m, d = x.shape
    _, tm, tf = _tiles(m)
    emit_weights = len(weights) == 2
    if emit_weights:
        assert m == tm
        w_up, w_down = weights
        d_ff = w_down.shape[1]
        nf = d_ff // tf
        operands = (w_up, w_up, w_down)
        w_specs = [_layer(lyr, (d, tf), lambda i, f: (0, f)), _layer(lyr, (d, tf), lambda i, f: (0, f + nf)),
                   _layer(lyr, (tf, d), lambda i, f: (f, 0))]
        out_specs = [pl.BlockSpec((d, tf), lambda i, f: (0, f)), pl.BlockSpec((d, tf), lambda i, f: (0, f)),
                     pl.BlockSpec((tf, d), lambda i, f: (f, 0))]
        out_shape = [jax.ShapeDtypeStruct((d, d_ff), BF16), jax.ShapeDtypeStruct((d, d_ff), BF16),
                     jax.ShapeDtypeStruct((d_ff, d), BF16)]
    else:
        operands = weights
        d_ff = weights[2].shape[0]
        nf = d_ff // tf
        w_specs = [pl.BlockSpec((d, tf), lambda i, f: (0, f)), pl.BlockSpec((d, tf), lambda i, f: (0, f)),
                   pl.BlockSpec((tf, d), lambda i, f: (f, 0))]
        out_specs, out_shape = [], []
    return pl.pallas_call(
        functools.partial(_ffn_kernel, final_norm=final_norm, emit_weights=emit_weights),
        grid=(m // tm, nf),
        in_specs=[pl.BlockSpec((tm, d), lambda i, f: (i, 0)), _whole(lyr, gains), *w_specs,
                  pl.BlockSpec((1, d), lambda i, f: (0, 0))],
        out_specs=[pl.BlockSpec((tm, d), lambda i, f: (i, 0)), *out_specs],
        out_shape=[jax.ShapeDtypeStruct((m, d), F32), *out_shape],
        scratch_shapes=[pltpu.VMEM((tm, d), BF16)],
        compiler_params=_params("parallel", "arbitrary"),
        name="ffn",
    )(x, gains, *operands, g_final)


def _even_prompt_kernel(a_ref, halo_ref, u_ref, v_ref, pw_ref, ps_ref, lg_ref, lb_ref, ws_ref, bst_ref,
                        x_ref, w0_ref, w1_ref, o_ref, buf_ref, oa_ref, ob_ref):
    t = pl.program_id(1)
    tt, ch = a_ref.shape
    gch = ch // len(POOL_WINDOWS)
    hch = ch // SGU_HEADS
    buf_ref[0:POOL_HALO, :] = jnp.where(t > 0, halo_ref[...], 0.0)
    buf_ref[POOL_HALO:, :] = a_ref[...]
    row = lax.broadcasted_iota(jnp.int32, (SGU_CHUNK, SGU_CHUNK), 0)
    col = lax.broadcasted_iota(jnp.int32, (SGU_CHUNK, SGU_CHUNK), 1)
    wm = [jnp.where(row >= col, ws_ref[h], 0.0).astype(BF16) for h in range(SGU_HEADS)]
    for c in range(tt // SGU_CHUNK):
        rows = slice(c * SGU_CHUNK, (c + 1) * SGU_CHUNK)
        a = a_ref[rows, :]
        pos = t * tt + c * SGU_CHUNK + lax.broadcasted_iota(jnp.int32, (SGU_CHUNK, 1), 0)
        for gi, w in enumerate(POOL_WINDOWS):
            cols = slice(gi * gch, (gi + 1) * gch)
            s = a[:, cols]
            for k in range(1, w):
                start = POOL_HALO + c * SGU_CHUNK - k
                s = s + buf_ref[start:start + SGU_CHUNK, cols]
            cnt = jnp.minimum(w, pos + 1).astype(F32)
            pooled = s / cnt - a[:, cols]
            mixed = _dot(pooled.astype(BF16), pw_ref[gi])
            oa_ref[rows, cols] = (mixed * ps_ref[:, cols]).astype(BF16)

        u = _gelu(u_ref[rows, :])
        vn = _layernorm_rows(_gelu(v_ref[rows, :]), lg_ref[...], lb_ref[...])
        for h in range(SGU_HEADS):
            cols = slice(h * hch, (h + 1) * hch)
            mixed = _dot(wm[h], vn[:, cols].astype(BF16)) + bst_ref[:, h:h + 1]
            ob_ref[rows, cols] = (u[:, cols] * mixed).astype(BF16)

        o_ref[rows, :] = x_ref[rows, :] + _dot(oa_ref[rows, :], w0_ref[...]) + _dot(ob_ref[rows, :], w1_ref[...])


def _even_prompt(z3, x3, lyr, pool_w, pool_scale, ln_g, ln_b, ws, bst, w_out, tt):
    b, t_len, _ = z3.shape
    ch = pool_scale.shape[-1]
    d = x3.shape[-1]
    hb = tt // POOL_HALO
    w_half = lambda r: pl.BlockSpec((None, ch, d), lambda i, t: (0, r, 0), pipeline_mode=pl.Buffered(1))
    rows = pl.BlockSpec((None, tt, d), lambda i, t: (i, t, 0))
    return pl.pallas_call(
        _even_prompt_kernel,
        grid=(b, t_len // tt),
        in_specs=[
            pl.BlockSpec((None, tt, ch), lambda i, t: (i, t, 0)),
            pl.BlockSpec((None, POOL_HALO, ch), lambda i, t: (i, jnp.maximum(t * hb - 1, 0), 0)),
            pl.BlockSpec((None, tt, ch), lambda i, t: (i, t, 1)),
            pl.BlockSpec((None, tt, ch), lambda i, t: (i, t, 2)),
            _whole(lyr, pool_w), _whole(lyr, pool_scale), _whole(lyr, ln_g), _whole(lyr, ln_b),
            _whole(lyr, ws), _whole(lyr, bst),
            rows, w_half(0), w_half(1),
        ],
        out_specs=rows,
        out_shape=jax.ShapeDtypeStruct(x3.shape, F32),
        scratch_shapes=[pltpu.VMEM((POOL_HALO + tt, ch), F32), pltpu.VMEM((tt, ch), BF16),
                        pltpu.VMEM((tt, ch), BF16)],
        compiler_params=_params("parallel", "arbitrary"),
        name="even_prompt",
    )(z3, z3, z3, z3, pool_w, pool_scale, ln_g, ln_b, ws, bst, x3, w_out, w_out)


def _even_sample_kernel(a_ref, u_ref, v_ref, st_ref, pw_ref, ps_ref, lg_ref, lb_ref, w0_ref, b0_ref,
                        oa_ref, ob_ref, nst_ref, nv_ref):
    ch = a_ref.shape[1]
    gch = ch // len(POOL_WINDOWS)
    a = a_ref[...]
    for gi, w in enumerate(POOL_WINDOWS):
        cols = slice(gi * gch, (gi + 1) * gch)
        s = a[:, cols]
        for k in range(1, w):
            s = s + st_ref[POOL_HIST - k, :, cols]
        pooled = s / float(min(w, PAST_LEN + 1)) - a[:, cols]
        mixed = _dot(pooled.astype(BF16), pw_ref[gi])
        oa_ref[:, cols] = (mixed * ps_ref[:, cols]).astype(BF16)
    nst_ref[0:POOL_HIST - 1] = st_ref[1:POOL_HIST]
    nst_ref[POOL_HIST - 1] = a

    u = _gelu(u_ref[...])
    vn = _layernorm_rows(_gelu(v_ref[...]), lg_ref[...], lb_ref[...])
    nv_ref[...] = vn
    ob_ref[...] = (u * (vn * w0_ref[...] + b0_ref[...])).astype(BF16)


def _even_sample(z, lyr, state, pool_w, pool_scale, ln_g, ln_b, w0, b0, carried, bt):
    assert PAST_LEN % SGU_CHUNK == 0
    n = z.shape[0]
    ch = pool_scale.shape[-1]
    act = jax.ShapeDtypeStruct((n, ch), BF16)
    rows = pl.BlockSpec((bt, ch), lambda i: (i, 0))
    n_in = 10
    stacked = {2: ((POOL_HIST, bt, ch), lambda i: (0, i, 0)), 3: ((bt, ch), lambda i: (i, 0))}
    call, extra = _stacked_call(
        _even_sample_kernel, n_in, carried, stacked, lyr,
        grid=(n // bt,),
        in_specs=[
            rows, pl.BlockSpec((bt, ch), lambda i: (i, 1)), pl.BlockSpec((bt, ch), lambda i: (i, 2)),
            _layer(lyr, (POOL_HIST, bt, ch), lambda i: (0, i, 0)),
            _whole(lyr, pool_w), _whole(lyr, pool_scale), _whole(lyr, ln_g), _whole(lyr, ln_b),
            _whole(lyr, w0), _whole(lyr, b0),
        ],
        out_specs=[rows, rows, None, None],
        out_shape=[act, act, jax.ShapeDtypeStruct(state.shape, F32),
                   jax.ShapeDtypeStruct((state.shape[0], n, ch), F32)],
        compiler_params=_params("parallel"),
        name="even_sample",
    )
    return call(z, z, z, state, pool_w, pool_scale, ln_g, ln_b, w0, b0, *extra)


def _conv_prompt_kernel(p_ref, q_ref, hp_ref, hq_ref, dw_ref, db_ref, lg_ref, lb_ref, o_ref, st_ref, buf_ref,
                        rot_ref, y_ref, wb_ref):
    t = pl.program_id(1)
    tt = p_ref.shape[0]
    hist = CONV_WIDTH - 1
    glu = p_ref[...] * jax.nn.sigmoid(q_ref[...])
    halo = hp_ref[...] * jax.nn.sigmoid(hq_ref[...])
    buf_ref[0:CONV_HALO, :] = jnp.where(t > 0, halo, 0.0)
    buf_ref[CONV_HALO:, :] = glu
    span = rot_ref.shape[1]
    for r in range(1, SUBLANES):
        rot_ref[r - 1] = buf_ref[r:r + span, :]
    base = CONV_HALO - hist
    for k in range(CONV_WIDTH):
        wb_ref[k] = jnp.broadcast_to(dw_ref[k:k + 1, :], wb_ref.shape[1:])
    subs = range(CONV_ROWS // SUBLANES)
    lane_chunks = p_ref.shape[1] // CONV_LANES

    def chunk(i, carry):
        r0 = pl.multiple_of((i // lane_chunks) * CONV_ROWS, CONV_ROWS)
        lanes = pl.ds(pl.multiple_of((i % lane_chunks) * CONV_LANES, CONV_LANES), CONV_LANES)
        at = lambda src, off, j: src[pl.ds(off + r0 + j * SUBLANES, SUBLANES), lanes]
        bias = db_ref[:, lanes]
        w = wb_ref[hist, :, lanes]
        acc = [bias + w * at(buf_ref, CONV_HALO, j) for j in subs]
        for k in range(hist):
            r = (base + k) % SUBLANES
            a = base + k - r
            src = buf_ref if r == 0 else rot_ref.at[r - 1]
            w = wb_ref[k, :, lanes]
            acc = [acc[j] + w * at(src, a, j) for j in subs]
        for j in subs:
            y_ref[pl.ds(r0 + j * SUBLANES, SUBLANES), lanes] = acc[j]
        return carry

    lax.fori_loop(0, (tt // CONV_ROWS) * lane_chunks, chunk, 0, unroll=2)
    o_ref[...] = _silu(_layernorm_rows(y_ref[...], lg_ref[...], lb_ref[...])).astype(BF16)

    @pl.when(t == pl.num_programs(1) - 1)
    def _():
        st_ref[...] = buf_ref[CONV_HALO + tt - hist:CONV_HALO + tt, :]


def _conv_prompt(z3, lyr, n_lyr, dw, db, ln_g, ln_b, carried, tt):
    b, t_len, _ = z3.shape
    ch = db.shape[-1]
    hb = tt // CONV_HALO
    hist = CONV_WIDTH - 1
    halo = lambda c: pl.BlockSpec((None, CONV_HALO, ch), lambda i, t: (i, jnp.maximum(t * hb - 1, 0), c))
    stacked = {1: ((None, hist, ch), lambda i, t: (i, 0, 0))}
    call, extra = _stacked_call(
        _conv_prompt_kernel, 8, carried, stacked, lyr,
        grid=(b, t_len // tt),
        in_specs=[
            pl.BlockSpec((None, tt, ch), lambda i, t: (i, t, 0)),
            pl.BlockSpec((None, tt, ch), lambda i, t: (i, t, 1)),
            halo(0), halo(1),
            _whole(lyr, dw), _whole(lyr, db), _whole(lyr, ln_g), _whole(lyr, ln_b),
        ],
        out_specs=[
            pl.BlockSpec((None, tt, ch), lambda i, t: (i, t, 0)),
            None,
        ],
        out_shape=[jax.ShapeDtypeStruct((b, t_len, ch), BF16), jax.ShapeDtypeStruct((n_lyr, b, hist, ch), F32)],
        scratch_shapes=[pltpu.VMEM((CONV_HALO + tt, ch), F32),
                        pltpu.VMEM((SUBLANES - 1, CONV_HALO + tt - SUBLANES, ch), F32),
                        pltpu.VMEM((tt, ch), F32), pltpu.VMEM((CONV_WIDTH, SUBLANES, ch), F32)],
        compiler_params=_params("parallel", "arbitrary"),
        name="conv_prompt",
    )
    return call(z3, z3, z3, z3, dw, db, ln_g, ln_b, *extra)


def _l2norm_heads(x, scale):
    outs = []
    for h in range(x.shape[1] // DN_DK):
        xh = x[:, h * DN_DK:(h + 1) * DN_DK]
        outs.append(xh * (lax.rsqrt(jnp.sum(xh * xh, -1, keepdims=True) + EPS) * scale))
    return jnp.concatenate(outs, -1)


def _dn_short_conv(raw_refs, rows, w_ref, first, tail_ref, buf_ref, out_ref):
    hist = DN_CONV - 1
    base = DN_HALO - hist
    tt = rows.stop - rows.start
    for part, (x_ref, scale) in enumerate(zip(raw_refs, (DN_DK ** -0.5, 1.0, None))):
        ch = x_ref.shape[1]
        cols = slice(part * ch, (part + 1) * ch)
        x = x_ref[rows, :]
        buf_ref[0:DN_HALO, :] = jnp.where(first, 0.0, tail_ref[part])
        buf_ref[DN_HALO:, :] = x
        tail_ref[part] = x[tt - DN_HALO:, :]
        acc = w_ref[hist:hist + 1, cols] * x
        for k in range(hist):
            acc = acc + w_ref[k:k + 1, cols] * buf_ref[base + k:base + k + tt, :]
        y = _silu(acc)
        out_ref[part, rows, :] = y if scale is None else _l2norm_heads(y, scale)


def _gate_terms(ab, alog, dtb):
    g = -jnp.exp(alog) * jax.nn.softplus(ab + dtb)
    return g, jax.nn.sigmoid(ab)


def _each(fn, *lists):
    return [fn(*xs) for xs in zip(*lists)]


def _delta_chunk_terms(q, k, v, gi, gj, bi, masks):
    lower, strict, same_block = masks
    n = q[0].shape[0]
    decay = _each(lambda a, b: jnp.where(lower, jnp.exp(jnp.where(lower, a - b, 0.0)), 0.0), gi, gj)
    eg = _each(jnp.exp, gi)
    kb = _each(jnp.multiply, k, bi)
    kk = _each(_bdot_nt, kb, k)
    amat = _each(lambda a, d: jnp.where(strict, a * d, 0.0), kk, decay)
    ad = _each(lambda a: jnp.where(same_block, a, 0.0), amat)
    an = _each(jnp.subtract, amat, ad)
    doff = _each(jnp.negative, ad)
    pw = ad
    for _ in range(DN_SOLVE_BLOCK.bit_length() - 2):
        pw = _each(_bdot, pw, pw)
        dp = _each(_bdot, doff, pw)
        doff = _each(lambda d, p, x: d + p + x, doff, pw, dp)
    assert n // DN_SOLVE_BLOCK == 4
    bm = _each(jnp.add, an, _each(_bdot, doff, an))
    rhs = _each(lambda vv, b, kk_, e: jnp.concatenate([vv * b, kk_ * e], -1), v, bi, kb, eg)
    sol = _each(jnp.add, rhs, _each(_bdot, doff, rhs))
    b2 = _each(_bdot, bm, bm)
    sol = _each(jnp.add, sol, _each(_bdot, b2, sol))
    sol = _each(jnp.subtract, sol, _each(_bdot, bm, sol))
    qk = _each(jnp.multiply, _each(_bdot_nt, q, k), decay)
    g_last = _each(lambda a: a[n - 1:n, :], gi)
    return (_each(lambda x: x[:, :DN_DK], sol), _each(lambda x: x[:, DN_DK:], sol), qk, _each(jnp.multiply, q, eg),
            _each(lambda kk_, gl, a: kk_ * jnp.exp(gl - a), k, g_last, gi), _each(jnp.exp, g_last))


def _delta_prompt_kernel(q_raw, k_raw, v_raw, zg_ref, ab_ref, cw_ref, alog_ref, dtb_ref, ng_ref, o_ref, sfin_ref,
                         s_ref, tail_ref, buf_ref, qkv_ref):
    c = pl.program_id(1)
    n = DN_CHUNK
    heads = range(DN_HEADS)
    chunks = range(q_raw.shape[0] // n)

    @pl.when(c == 0)
    def _():
        s_ref[...] = jnp.zeros_like(s_ref)

    groups = [chunks[g:g + DN_PREP_CHUNKS] for g in range(0, len(chunks), DN_PREP_CHUNKS)]
    for g, group in enumerate(groups):
        span = slice(group[0] * n, (group[-1] + 1) * n)
        _dn_short_conv((q_raw, k_raw, v_raw), span, cw_ref, (c == 0) if g == 0 else False, tail_ref, buf_ref,
                       qkv_ref)
    q_ref, k_ref, v_ref = qkv_ref.at[0], qkv_ref.at[1], qkv_ref.at[2]

    row = lax.broadcasted_iota(jnp.int32, (n, n), 0)
    col = lax.broadcasted_iota(jnp.int32, (n, n), 1)
    lower = row >= col
    masks = (lower, row > col, (row // DN_SOLVE_BLOCK) == (col // DN_SOLVE_BLOCK))
    tri = lower.astype(F32)
    rows = [slice(j * n, (j + 1) * n) for j in chunks]
    cols = [slice(h * DN_DK, (h + 1) * DN_DK) for h in heads]
    gate = [_gate_terms(ab_ref[r, :], alog_ref[...], dtb_ref[...]) for r in rows]
    gcol = [_dot_hi(tri, g) for g, _ in gate]
    grow = [_dot_nt_hi(g.T, tri) for g, _ in gate]
    terms = []
    for group in groups:
        units = [(j, h) for j in group for h in heads]
        terms.append(_delta_chunk_terms(
            [q_ref[rows[j], cols[h]] for j, h in units], [k_ref[rows[j], cols[h]] for j, h in units],
            [v_ref[rows[j], cols[h]] for j, h in units], [gcol[j][:, h:h + 1] for j, h in units],
            [grow[j][h:h + 1, :] for j, h in units],
            [gate[j][1][:, DN_HEADS + h:DN_HEADS + h + 1] for j, h in units], masks))
    u, w, qk, qd, kd, gl = (sum((list(t[i]) for t in terms), []) for i in range(6))
    s = [s_ref[h] for h in heads]
    for j in chunks:
        at = lambda xs: xs[j * DN_HEADS:(j + 1) * DN_HEADS]
        v_new = _each(jnp.subtract, at(u), _each(_bdot, at(w), s))
        o = _each(jnp.add, _each(_bdot, at(qd), s), _each(_bdot, at(qk), v_new))
        s = _each(lambda ss, g, x: ss * g + x, s, at(gl), _each(_bdot_tn, at(kd), v_new))
        for h in heads:
            on = o[h] * lax.rsqrt(jnp.mean(o[h] * o[h], -1, keepdims=True) + EPS) * ng_ref[...]
            o_ref[rows[j], cols[h]] = (on * _silu(zg_ref[rows[j], cols[h]])).astype(BF16)
    for h in heads:
        s_ref[h] = s[h]

    @pl.when(c == pl.num_programs(1) - 1)
    def _():
        sfin_ref[...] = s_ref[...]


def _delta_prompt(z3, qkv_col, zg_col, ab3, lyr, n_lyr, conv_w, a_log, dt_bias, norm_g, carried):
    b, t_len, _ = z3.shape
    ch = DN_HEADS * DN_DK
    n = DN_CHUNK * DN_STEP_CHUNKS
    zcol = lambda col: pl.BlockSpec((None, n, ch), lambda i, c: (i, c, col))
    stacked = {1: ((None, DN_HEADS, DN_DK, DN_DK), lambda i, c: (i, 0, 0, 0))}
    call, extra = _stacked_call(
        _delta_prompt_kernel, 9, carried, stacked, lyr,
        grid=(b, t_len // n),
        in_specs=[
            zcol(qkv_col), zcol(qkv_col + 1), zcol(qkv_col + 2), zcol(zg_col),
            pl.BlockSpec((None, n, ab3.shape[-1]), lambda i, c: (i, c, 0)),
            _whole(lyr, conv_w), _whole(lyr, a_log), _whole(lyr, dt_bias), _whole(lyr, norm_g),
        ],
        out_specs=[
            pl.BlockSpec((None, n, ch), lambda i, c: (i, c, 0)),
            None,
        ],
        out_shape=[jax.ShapeDtypeStruct((b, t_len, ch), BF16),
                   jax.ShapeDtypeStruct((n_lyr, b, DN_HEADS, DN_DK, DN_DK), F32)],
        scratch_shapes=[pltpu.VMEM((DN_HEADS, DN_DK, DN_DK), F32), pltpu.VMEM((3, DN_HALO, ch), F32),
                        pltpu.VMEM((DN_HALO + DN_CHUNK * DN_PREP_CHUNKS, ch), F32), pltpu.VMEM((3, n, ch), F32)],
        compiler_params=_params("parallel", "arbitrary"),
        name="delta_prompt",
    )
    return call(z3, z3, z3, z3, ab3, conv_w, a_log, dt_bias, norm_g, *extra)


def _odd_sample_kernel(p_ref, gt_ref, q_ref, k_ref, v_ref, zg_ref, ab_ref, cst_ref, dst_ref, s_ref,
                       dw_ref, db_ref, lg_ref, lb_ref, cw_ref, alog_ref, dtb_ref, ng_ref,
                       oc_ref, od_ref, ncst_ref, ndst_ref, ns_ref, obuf_ref):
    bt, ch = p_ref.shape
    hist = CONV_WIDTH - 1
    glu = p_ref[...] * jax.nn.sigmoid(gt_ref[...])
    acc = db_ref[...] + dw_ref[hist:hist + 1, :] * glu
    for k in range(hist):
        acc = acc + dw_ref[k:k + 1, :] * cst_ref[k]
    oc_ref[...] = _silu(_layernorm_rows(acc, lg_ref[...], lb_ref[...])).astype(BF16)
    ncst_ref[0:hist - 1] = cst_ref[1:hist]
    ncst_ref[hist - 1] = glu

    dh = DN_CONV - 1
    parts = []
    for pi, x_ref in enumerate((q_ref, k_ref, v_ref)):
        cols = slice(pi * ch, (pi + 1) * ch)
        x = x_ref[...]
        y = cw_ref[dh:dh + 1, cols] * x
        for k in range(dh):
            y = y + cw_ref[k:k + 1, cols] * dst_ref[k, :, cols]
        parts.append(_silu(y))
        ndst_ref[0:dh - 1, :, cols] = dst_ref[1:dh, :, cols]
        ndst_ref[dh - 1, :, cols] = x
    q = _l2norm_heads(parts[0], DN_DK ** -0.5)
    k = _l2norm_heads(parts[1], 1.0)
    v = parts[2]
    g, beta = _gate_terms(ab_ref[...], alog_ref[...], dtb_ref[...])
    eg_all = jnp.exp(g)

    assert 2 * DN_HEADS * bt == LANES
    kq = jnp.concatenate([x[:, h * DN_DK:(h + 1) * DN_DK] for x in (k, q) for h in range(DN_HEADS)], 0)
    kq_t = kq.T
    for h in range(DN_HEADS):
        cols = slice(h * DN_DK, (h + 1) * DN_DK)
        for b in range(bt):
            kc = kq_t[:, h * bt + b:h * bt + b + 1]
            qc = kq_t[:, (DN_HEADS + h) * bt + b:(DN_HEADS + h) * bt + b + 1]
            s = s_ref[b, h]
            sk = jnp.sum(s * kc, 0, keepdims=True)
            sq = jnp.sum(s * qc, 0, keepdims=True)
            eg = eg_all[b:b + 1, h:h + 1]
            bi = beta[b:b + 1, DN_HEADS + h:DN_HEADS + h + 1]
            v_new = bi * (v[b:b + 1, cols] - eg * sk)
            qk = jnp.sum(q[b:b + 1, cols] * k[b:b + 1, cols], -1, keepdims=True)
            obuf_ref[b:b + 1, cols] = eg * sq + qk * v_new
            ns_ref[b, h] = s * eg + kc * v_new
    for h in range(DN_HEADS):
        cols = slice(h * DN_DK, (h + 1) * DN_DK)
        o = obuf_ref[:, cols]
        o = o * lax.rsqrt(jnp.mean(o * o, -1, keepdims=True) + EPS) * ng_ref[...]
        od_ref[:, cols] = (o * _silu(zg_ref[:, cols])).astype(BF16)


def _odd_sample(z, ab, lyr, cstate, dstate, sstate, dw, db, ln_g, ln_b, conv_w, a_log, dt_bias, norm_g, carried, bt):
    n = z.shape[0]
    ch = db.shape[-1]
    zcol = lambda c: pl.BlockSpec((bt, ch), lambda i: (i, c))
    lead_blk = lambda a: ((bt,) + a.shape[2:], lambda i: (i,) + (0,) * (a.ndim - 2))
    hist_blk = lambda a: ((a.shape[1], bt, a.shape[3]), lambda i: (0, i, 0))
    lead = lambda a: _layer(lyr, *lead_blk(a))
    hist = lambda a: _layer(lyr, *hist_blk(a))
    act = jax.ShapeDtypeStruct((n, ch), BF16)
    same = lambda a: jax.ShapeDtypeStruct(a.shape, F32)
    stacked = {2: hist_blk(cstate), 3: hist_blk(dstate), 4: lead_blk(sstate)}
    call, extra = _stacked_call(
        _odd_sample_kernel, 18, carried, stacked, lyr,
        grid=(n // bt,),
        in_specs=[zcol(0), zcol(1), zcol(2), zcol(3), zcol(4), zcol(5),
                  pl.BlockSpec((bt, ab.shape[-1]), lambda i: (i, 0)),
                  hist(cstate), hist(dstate), lead(sstate),
                  _whole(lyr, dw), _whole(lyr, db), _whole(lyr, ln_g), _whole(lyr, ln_b),
                  _whole(lyr, conv_w), _whole(lyr, a_log), _whole(lyr, dt_bias), _whole(lyr, norm_g)],
        out_specs=[zcol(0), zcol(0), None, None, None],
        out_shape=[act, act, same(cstate), same(dstate), same(sstate)],
        scratch_shapes=[pltpu.VMEM((bt, ch), F32)],
        compiler_params=_params("parallel"),
        name="odd_sample",
    )
    return call(z, z, z, z, z, z, ab, cstate, dstate, sstate, dw, db, ln_g, ln_b, conv_w, a_log, dt_bias, norm_g,
                *extra)


def kernel(x_prompt, x_sample, state_pool, state_conv_c, state_dn_conv, state_dn_S, norm_mix, norm_ffn, norm_final, ev_w_in, pool_w, pool_scale, sgu_ln_g, sgu_ln_b, sgu_ws, sgu_b, ev_w_out, od_w_in, cv_dw, cv_db, cv_ln_g, cv_ln_b, dn_conv_w, dn_a_log, dn_dt_bias, dn_norm_g, od_w_out, ffn_w_up, ffn_w_down):
    bp, t_len, d = x_prompt.shape
    bs = x_sample.shape[0]
    assert x_sample.shape[1] == 1
    depth = norm_mix.shape[0]
    n_even, n_odd = ev_w_in.shape[0], od_w_in.shape[0]
    ch = pool_scale.shape[1]
    odd_main = 6 * ch
    n_gate = 2 * DN_HEADS
    assert od_w_in.shape[2] == odd_main + n_gate and odd_main % n_gate == 0

    rows = lambda a: a.reshape(a.shape[0], 1, a.shape[-1])
    gate_rows = lambda a: rows(jnp.pad(a, ((0, 0), (0, DN_HEADS))))
    od_in = jnp.swapaxes(od_w_in, 1, 2)
    pool_wb = pool_w.astype(BF16)
    g_mix, g_ffn, g_fin = rows(norm_mix), rows(norm_ffn), norm_final.reshape(1, d)
    p_scale, s_lg, s_lb = rows(pool_scale), rows(sgu_ln_g), rows(sgu_ln_b)
    s_bt = jnp.swapaxes(sgu_b, 1, 2)
    s_w0 = rows(jnp.repeat(sgu_ws[:, :, 0, 0], ch // SGU_HEADS, axis=1))
    s_b0 = rows(jnp.repeat(sgu_b[:, :, 0], ch // SGU_HEADS, axis=1))
    c_db, c_lg, c_lb = rows(cv_db), rows(cv_ln_g), rows(cv_ln_b)
    a_log, dt_bias, d_ng = gate_rows(dn_a_log), gate_rows(dn_dt_bias), rows(dn_norm_g)

    hist_major = lambda a: jnp.swapaxes(a, 1, 2)
    st_pool, st_cc, st_dc = hist_major(state_pool), hist_major(state_conv_c), hist_major(state_dn_conv)

    xp = x_prompt.reshape(bp * t_len, d)
    xs = x_sample.reshape(bs, d)
    tn = 2 * _tiles(bp * t_len)[2]
    flat = lambda a: a.reshape(bp * t_len, a.shape[-1])
    pool_p, dc_p = [], []
    even_s = cc_p = s_p = odd_s = None
    for l in range(depth):
        i = l // 2
        if l % 2 == 0:
            zs, w_in = _norm_matmul(xs, g_mix, l, ev_w_in, i, 3 * ch, tn, emit_weights=True)
            zp, = _norm_matmul(xp, g_mix, l, w_in, 0, 3 * ch, tn)
            zp3 = zp.reshape(bp, t_len, 3 * ch)
            a_s, b_s, *even_s = _even_sample(zs, i, st_pool, pool_wb, p_scale, s_lg, s_lb, s_w0, s_b0, even_s, 8)
            pool_p.append(zp3[:, t_len - POOL_HIST:, :ch])
            xs, w_out = _mix_out(a_s, b_s, ev_w_out, i, xs, emit_weights=True)
            xp = flat(_even_prompt(zp3, xp.reshape(bp, t_len, d), i, pool_wb, p_scale, s_lg, s_lb, sgu_ws, s_bt,
                                   w_out, 512))
        else:
            zp, abp = _norm_matmul(xp, g_mix, l, od_in, i, odd_main, tn, True, n_gate)
            zs, abs_ = _norm_matmul(xs, g_mix, l, od_in, i, odd_main, tn, True, n_gate)
            zp3 = zp.reshape(bp, t_len, odd_main)
            c_p, *cc_p = _conv_prompt(zp3, i, n_odd, cv_dw, c_db, c_lg, c_lb, cc_p, 256)
            d_p, *s_p = _delta_prompt(zp3, 2, 5, abp.reshape(bp, t_len, n_gate), i, n_odd, dn_conv_w, a_log, dt_bias,
                                      d_ng, s_p)
            c_s, d_s, *odd_s = _odd_sample(zs, abs_, i, st_cc, st_dc, state_dn_S, cv_dw, c_db, c_lg,
                                           c_lb, dn_conv_w, a_log, dt_bias, d_ng, odd_s, 8)
            dc_p.append(zp3[:, t_len - (DN_CONV - 1):, 2 * ch:5 * ch])
            xs, w_out = _mix_out(c_s, d_s, od_w_out, i, xs, emit_weights=True)
            xp = _mix_out(flat(c_p), flat(d_p), w_out, 0, xp)
        last = l == depth - 1
        xs, *w_ffn = _ffn(xs, g_ffn, l, (ffn_w_up, ffn_w_down), g_fin, last)
        xp, = _ffn(xp, g_ffn, l, w_ffn, g_fin, last)
    pool_s, v_s = even_s
    cc_s, dc_s, s_s = odd_s
    return (xp.reshape(bp, t_len, d), xs.reshape(bs, 1, d), jnp.stack(pool_p), hist_major(pool_s),
            v_s.reshape(n_even, bs, 1, ch), cc_p[0], hist_major(cc_s), jnp.stack(dc_p), hist_major(dc_s), s_p[0], s_s)
```

```python
import functools

import jax
import jax.numpy as jnp
from jax import lax
from jax.experimental import pallas as pl
from jax.experimental.pallas import tpu as pltpu

F32 = jnp.float32
BF16 = jnp.bfloat16
HI = lax.Precision.HIGHEST

EPS = 1e-6
PAST_LEN = 16384
POOL_WINDOWS = (2, 4, 8, 16)
POOL_HIST = max(POOL_WINDOWS) - 1
SGU_HEADS = 4
SGU_CHUNK = 128
CONV_WIDTH = 31
DN_HEADS = 8
DN_DK = 128
DN_CONV = 4
DN_CHUNK = 64
DN_SOLVE_BLOCK = 16
DN_STEP_CHUNKS = 4
DN_PREP_CHUNKS = 2

LANES = 128
SUBLANES = 8
VMEM_LIMIT = 56 * 1024 * 1024

POOL_HALO = 16
CONV_HALO = 32
DN_HALO = 8
CONV_ROWS, CONV_LANES = 64, 256


def _tiles(m):
    return min(m, 1024), min(m, 512), 512


def _params(*semantics):
    return pltpu.CompilerParams(dimension_semantics=semantics, vmem_limit_bytes=VMEM_LIMIT)


def _layer(lyr, block, index_map):
    return pl.BlockSpec((None,) + tuple(block), lambda *g: (lyr,) + tuple(index_map(*g)))


def _whole(lyr, arr):
    return pl.BlockSpec((None,) + arr.shape[1:], lambda *g: (lyr,) + (0,) * (arr.ndim - 1))


def _skip_refs(fn, start, count):
    def body(*refs):
        return fn(*refs[:start], *refs[start + count:])
    return body


def _own_layer_zero_rest(fn, n_in, outs, lyr):
    def body(*refs):
        refs = list(refs)
        for o in outs:
            full = refs[n_in + o]
            for other in range(full.shape[0]):
                if other != lyr:
                    full[other] = jnp.zeros(full.shape[1:], full.dtype)
            refs[n_in + o] = full.at[lyr]
        return fn(*refs)
    return body


def _stacked_call(kernel_fn, n_in, carried, stacked_out, lyr, **kw):
    out_specs = list(kw.pop("out_specs"))
    if carried is None:
        for o, (block, imap) in stacked_out.items():
            n_lyr = kw["out_shape"][o].shape[0]
            out_specs[o] = pl.BlockSpec((n_lyr,) + tuple(block), lambda *g, imap=imap: (0,) + tuple(imap(*g)))
        body = _own_layer_zero_rest(kernel_fn, n_in, tuple(stacked_out), lyr)
        return pl.pallas_call(body, out_specs=out_specs, **kw), ()
    for o, (block, imap) in stacked_out.items():
        out_specs[o] = _layer(lyr, block, imap)
    kw["in_specs"] = list(kw["in_specs"]) + [pl.BlockSpec(memory_space=pl.ANY)] * len(carried)
    aliases = {n_in + k: o for k, o in enumerate(stacked_out)}
    body = _skip_refs(kernel_fn, n_in, len(carried))
    return pl.pallas_call(body, out_specs=out_specs, input_output_aliases=aliases, **kw), tuple(carried)


def _rms_rows(x, g):
    return x * lax.rsqrt(jnp.mean(x * x, -1, keepdims=True) + EPS) * g


def _layernorm_rows(x, g, b):
    mu = jnp.mean(x, -1, keepdims=True)
    xc = x - mu
    var = jnp.mean(xc * xc, -1, keepdims=True)
    return xc * lax.rsqrt(var + EPS) * g + b


def _gelu(x):
    return 0.5 * x * (1.0 + lax.erf(x * 0.7071067811865476))


def _silu(x):
    return x * jax.nn.sigmoid(x)


def _dot(a, b):
    return jnp.dot(a, b, preferred_element_type=F32)


def _dot_hi(a, b):
    return jnp.dot(a, b, precision=HI, preferred_element_type=F32)


def _dot_nt_hi(a, b):
    return lax.dot_general(a, b, (((1,), (1,)), ((), ())), precision=HI, preferred_element_type=F32)


def _bdot(a, b):
    return jnp.dot(a.astype(BF16), b.astype(BF16), preferred_element_type=F32)


def _bdot_nt(a, b):
    return lax.dot_general(a.astype(BF16), b.astype(BF16), (((1,), (1,)), ((), ())), preferred_element_type=F32)


def _bdot_tn(a, b):
    return lax.dot_general(a.astype(BF16), b.astype(BF16), (((0,), (0,)), ((), ())), preferred_element_type=F32)


def _dot_nt(a, b):
    return lax.dot_general(a, b, (((1,), (1,)), ((), ())), preferred_element_type=F32)


def _norm_matmul_kernel(x_ref, g_ref, w_ref, *rest, w_rows_out, names):
    r = dict(zip(names, rest))
    o_ref, h_ref = r["out"], r["h"]

    @pl.when(pl.program_id(1) == 0)
    def _():
        h = _rms_rows(x_ref[...], g_ref[...]).astype(BF16)
        h_ref[...] = h
        if "w_tail" in r:
            wt = r["w_tail"][...].astype(BF16)
            r["out_tail"][...] = _dot_nt(h, wt)
            if "emit_tail" in r:
                r["emit_tail"][...] = wt

    w = w_ref[...].astype(BF16)
    if "emit" in r:
        r["emit"][...] = w
    o_ref[...] = _dot_nt(h_ref[...], w) if w_rows_out else _dot(h_ref[...], w)


def _norm_matmul(x, gains, lyr, w, wl, n_out, tn, w_rows_out=False, tail=None, emit_weights=False):
    m, d = x.shape
    tm = _tiles(m)[0]
    if w_rows_out:
        w_spec = _layer(wl, (tn, d), lambda i, j: (j, 0))
    else:
        assert tail is None
        w_spec = _layer(wl, (d, tn), lambda i, j: (0, j))
    in_specs = [pl.BlockSpec((tm, d), lambda i, j: (i, 0)), _whole(lyr, gains), w_spec]
    out_specs = [pl.BlockSpec((tm, tn), lambda i, j: (i, j))]
    out_shape = [jax.ShapeDtypeStruct((m, n_out), F32)]
    operands = [x, gains, w]
    names = ["out"]
    if tail is not None:
        tail_w, tail_l, tail_block, n_tail = tail
        in_specs.append(_layer(tail_l, (n_tail, d), lambda i, j: (tail_block, 0)))
        operands.append(tail_w)
        out_specs.append(pl.BlockSpec((tm, n_tail), lambda i, j: (i, 0)))
        out_shape.append(jax.ShapeDtypeStruct((m, n_tail), F32))
        names = ["w_tail", "out", "out_tail"]
    if emit_weights:
        assert m == tm
        w_block = (None, tn, d) if w_rows_out else (None, d, tn)
        out_specs.append(pl.BlockSpec(w_block, (lambda i, j: (0, j, 0)) if w_rows_out else (lambda i, j: (0, 0, j))))
        out_shape.append(jax.ShapeDtypeStruct((1, n_out, d) if w_rows_out else (1, d, n_out), BF16))
        names.append("emit")
        if tail is not None:
            out_specs.append(pl.BlockSpec((None, n_tail, d), lambda i, j: (0, 0, 0)))
            out_shape.append(jax.ShapeDtypeStruct((1, n_tail, d), BF16))
            names.append("emit_tail")
    names.append("h")
    return pl.pallas_call(
        functools.partial(_norm_matmul_kernel, w_rows_out=w_rows_out, names=tuple(names)),
        grid=(m // tm, n_out // tn),
        in_specs=in_specs,
        out_specs=out_specs,
        out_shape=out_shape,
        scratch_shapes=[pltpu.VMEM((tm, d), BF16)],
        compiler_params=_params("parallel", "arbitrary"),
        name="norm_matmul",
    )(*operands)


def _mix_out_kernel(a0_ref, a1_ref, w0_ref, w1_ref, x_ref, o_ref, *w_out):
    w0, w1 = w0_ref[...].astype(BF16), w1_ref[...].astype(BF16)
    if w_out:
        w_out[0][0], w_out[0][1] = w0, w1
    o_ref[...] = x_ref[...] + _dot(a0_ref[...], w0) + _dot(a1_ref[...], w1)


def _mix_out(a0, a1, w, wl, x, emit_weights=False):
    m, kh = a0.shape
    n = w.shape[2]
    tm = _tiles(m)[1]
    assert not emit_weights or m == tm
    w_half = lambda r: pl.BlockSpec((None, kh, n), lambda i: (wl, r, 0), pipeline_mode=pl.Buffered(1))
    rows = pl.BlockSpec((tm, n), lambda i: (i, 0))
    out = pl.pallas_call(
        _mix_out_kernel,
        grid=(m // tm,),
        in_specs=[
            pl.BlockSpec((tm, kh), lambda i: (i, 0)),
            pl.BlockSpec((tm, kh), lambda i: (i, 0)),
            w_half(0), w_half(1),
            rows,
        ],
        out_specs=[rows] + [pl.BlockSpec((2, kh, n), lambda i: (0, 0, 0))] * emit_weights,
        out_shape=[jax.ShapeDtypeStruct((m, n), F32)] + [jax.ShapeDtypeStruct((2, kh, n), BF16)] * emit_weights,
        compiler_params=_params("parallel"),
        name="mix_out",
    )(a0, a1, w, w, x)
    return (out[0], out[1].reshape(1, 2 * kh, n)) if emit_weights else out[0]


def _ffn_kernel(x_ref, g_ref, wg_ref, wu_ref, wd_ref, gf_ref, o_ref, *rest, final_norm, emit_weights):
    h_ref = rest[-1]
    f = pl.program_id(1)

    @pl.when(f == 0)
    def _():
        x = x_ref[...]
        h_ref[...] = _rms_rows(x, g_ref[...]).astype(BF16)
        o_ref[...] = x

    wg, wu, wd = wg_ref[...].astype(BF16), wu_ref[...].astype(BF16), wd_ref[...].astype(BF16)
    if emit_weights:
        wg_out, wu_out, wd_out = rest[:3]
        wg_out[...] = wg
        wu_out[...] = wu
        wd_out[...] = wd
    h = h_ref[...]
    act = _silu(_dot(h, wg)) * _dot(h, wu)
    o_ref[...] += _dot(act.astype(BF16), wd)

    if final_norm:
        @pl.when(f == pl.num_programs(1) - 1)
        def _():
            o_ref[...] = _rms_rows(o_ref[...], gf_ref[...])


def _ffn(x, gains, lyr, weights, g_final, final_norm):
    m, d = x.shape
    _, tm, tf = _tiles(m)
    emit_weights = len(weights) == 2
    if emit_weights:
        assert m == tm
        w_up, w_down = weights
        d_ff = w_down.shape[1]
        nf = d_ff // tf
        operands = (w_up, w_up, w_down)
        w_specs = [_layer(lyr, (d, tf), lambda i, f: (0, f)), _layer(lyr, (d, tf), lambda i, f: (0, f + nf)),
                   _layer(lyr, (tf, d), lambda i, f: (f, 0))]
        out_specs = [pl.BlockSpec((d, tf), lambda i, f: (0, f)), pl.BlockSpec((d, tf), lambda i, f: (0, f)),
                     pl.BlockSpec((tf, d), lambda i, f: (f, 0))]
        out_shape = [jax.ShapeDtypeStruct((d, d_ff), BF16), jax.ShapeDtypeStruct((d, d_ff), BF16),
                     jax.ShapeDtypeStruct((d_ff, d), BF16)]
    else:
        operands = weights
        d_ff = weights[2].shape[0]
        nf = d_ff // tf
        w_specs = [pl.BlockSpec((d, tf), lambda i, f: (0, f)), pl.BlockSpec((d, tf), lambda i, f: (0, f)),
                   pl.BlockSpec((tf, d), lambda i, f: (f, 0))]
        out_specs, out_shape = [], []
    return pl.pallas_call(
        functools.partial(_ffn_kernel, final_norm=final_norm, emit_weights=emit_weights),
        grid=(m // tm, nf),
        in_specs=[pl.BlockSpec((tm, d), lambda i, f: (i, 0)), _whole(lyr, gains), *w_specs,
                  pl.BlockSpec((1, d), lambda i, f: (0, 0))],
        out_specs=[pl.BlockSpec((tm, d), lambda i, f: (i, 0)), *out_specs],
        out_shape=[jax.ShapeDtypeStruct((m, d), F32), *out_shape],
        scratch_shapes=[pltpu.VMEM((tm, d), BF16)],
        compiler_params=_params("parallel", "arbitrary"),
        name="ffn",
    )(x, gains, *operands, g_final)


def _even_prompt_kernel(a_ref, halo_ref, u_ref, v_ref, pw_ref, ps_ref, lg_ref, lb_ref, ws_ref, bst_ref,
                        x_ref, w0_ref, w1_ref, o_ref, buf_ref, oa_ref, ob_ref):
    t = pl.program_id(1)
    tt, ch = a_ref.shape
    gch = ch // len(POOL_WINDOWS)
    hch = ch // SGU_HEADS
    buf_ref[0:POOL_HALO, :] = jnp.where(t > 0, halo_ref[...], 0.0)
    buf_ref[POOL_HALO:, :] = a_ref[...]
    row = lax.broadcasted_iota(jnp.int32, (SGU_CHUNK, SGU_CHUNK), 0)
    col = lax.broadcasted_iota(jnp.int32, (SGU_CHUNK, SGU_CHUNK), 1)
    wm = [jnp.where(row >= col, ws_ref[h], 0.0).astype(BF16) for h in range(SGU_HEADS)]
    for c in range(tt // SGU_CHUNK):
        rows = slice(c * SGU_CHUNK, (c + 1) * SGU_CHUNK)
        a = a_ref[rows, :]
        pos = t * tt + c * SGU_CHUNK + lax.broadcasted_iota(jnp.int32, (SGU_CHUNK, 1), 0)
        for gi, w in enumerate(POOL_WINDOWS):
            cols = slice(gi * gch, (gi + 1) * gch)
            s = a[:, cols]
            for k in range(1, w):
                start = POOL_HALO + c * SGU_CHUNK - k
                s = s + buf_ref[start:start + SGU_CHUNK, cols]
            cnt = jnp.minimum(w, pos + 1).astype(F32)
            pooled = s / cnt - a[:, cols]
            mixed = _dot(pooled.astype(BF16), pw_ref[gi])
            oa_ref[rows, cols] = (mixed * ps_ref[:, cols]).astype(BF16)

        u = _gelu(u_ref[rows, :])
        vn = _layernorm_rows(_gelu(v_ref[rows, :]), lg_ref[...], lb_ref[...])
        for h in range(SGU_HEADS):
            cols = slice(h * hch, (h + 1) * hch)
            mixed = _dot(wm[h], vn[:, cols].astype(BF16)) + bst_ref[:, h:h + 1]
            ob_ref[rows, cols] = (u[:, cols] * mixed).astype(BF16)

        o_ref[rows, :] = x_ref[rows, :] + _dot(oa_ref[rows, :], w0_ref[...]) + _dot(ob_ref[rows, :], w1_ref[...])


def _even_prompt(z3, x3, lyr, pool_w, pool_scale, ln_g, ln_b, ws, bst, w_out, tt):
    b, t_len, _ = z3.shape
    ch = pool_scale.shape[-1]
    d = x3.shape[-1]
    hb = tt // POOL_HALO
    w_half = lambda r: pl.BlockSpec((None, ch, d), lambda i, t: (0, r, 0), pipeline_mode=pl.Buffered(1))
    rows = pl.BlockSpec((None, tt, d), lambda i, t: (i, t, 0))
    return pl.pallas_call(
        _even_prompt_kernel,
        grid=(b, t_len // tt),
        in_specs=[
            pl.BlockSpec((None, tt, ch), lambda i, t: (i, t, 0)),
            pl.BlockSpec((None, POOL_HALO, ch), lambda i, t: (i, jnp.maximum(t * hb - 1, 0), 0)),
            pl.BlockSpec((None, tt, ch), lambda i, t: (i, t, 1)),
            pl.BlockSpec((None, tt, ch), lambda i, t: (i, t, 2)),
            _whole(lyr, pool_w), _whole(lyr, pool_scale), _whole(lyr, ln_g), _whole(lyr, ln_b),
            _whole(lyr, ws), _whole(lyr, bst),
            rows, w_half(0), w_half(1),
        ],
        out_specs=rows,
        out_shape=jax.ShapeDtypeStruct(x3.shape, F32),
        scratch_shapes=[pltpu.VMEM((POOL_HALO + tt, ch), F32), pltpu.VMEM((tt, ch), BF16),
                        pltpu.VMEM((tt, ch), BF16)],
        compiler_params=_params("parallel", "arbitrary"),
        name="even_prompt",
    )(z3, z3, z3, z3, pool_w, pool_scale, ln_g, ln_b, ws, bst, x3, w_out, w_out)


def _even_sample_kernel(a_ref, u_ref, v_ref, st_ref, pw_ref, ps_ref, lg_ref, lb_ref, w0_ref, b0_ref,
                        oa_ref, ob_ref, nst_ref, nv_ref):
    ch = a_ref.shape[1]
    gch = ch // len(POOL_WINDOWS)
    a = a_ref[...]
    for gi, w in enumerate(POOL_WINDOWS):
        cols = slice(gi * gch, (gi + 1) * gch)
        s = a[:, cols]
        for k in range(1, w):
            s = s + st_ref[POOL_HIST - k, :, cols]
        pooled = s / float(min(w, PAST_LEN + 1)) - a[:, cols]
        mixed = _dot(pooled.astype(BF16), pw_ref[gi])
        oa_ref[:, cols] = (mixed * ps_ref[:, cols]).astype(BF16)
    nst_ref[0:POOL_HIST - 1] = st_ref[1:POOL_HIST]
    nst_ref[POOL_HIST - 1] = a

    u = _gelu(u_ref[...])
    vn = _layernorm_rows(_gelu(v_ref[...]), lg_ref[...], lb_ref[...])
    nv_ref[...] = vn
    ob_ref[...] = (u * (vn * w0_ref[...] + b0_ref[...])).astype(BF16)


def _even_sample(z, lyr, state, pool_w, pool_scale, ln_g, ln_b, w0, b0, carried, bt):
    assert PAST_LEN % SGU_CHUNK == 0
    n = z.shape[0]
    ch = pool_scale.shape[-1]
    act = jax.ShapeDtypeStruct((n, ch), BF16)
    rows = pl.BlockSpec((bt, ch), lambda i: (i, 0))
    n_in = 10
    stacked = {2: ((POOL_HIST, bt, ch), lambda i: (0, i, 0)), 3: ((bt, ch), lambda i: (i, 0))}
    call, extra = _stacked_call(
        _even_sample_kernel, n_in, carried, stacked, lyr,
        grid=(n // bt,),
        in_specs=[
            rows, pl.BlockSpec((bt, ch), lambda i: (i, 1)), pl.BlockSpec((bt, ch), lambda i: (i, 2)),
            _layer(lyr, (POOL_HIST, bt, ch), lambda i: (0, i, 0)),
            _whole(lyr, pool_w), _whole(lyr, pool_scale), _whole(lyr, ln_g), _whole(lyr, ln_b),
            _whole(lyr, w0), _whole(lyr, b0),
        ],
        out_specs=[rows, rows, None, None],
        out_shape=[act, act, jax.ShapeDtypeStruct(state.shape, F32),
                   jax.ShapeDtypeStruct((state.shape[0], n, ch), F32)],
        compiler_params=_params("parallel"),
        name="even_sample",
    )
    return call(z, z, z, state, pool_w, pool_scale, ln_g, ln_b, w0, b0, *extra)


def _conv_prompt_kernel(p_ref, q_ref, hp_ref, hq_ref, dw_ref, db_ref, lg_ref, lb_ref, o_ref, st_ref, buf_ref,
                        rot_ref, y_ref, wb_ref):
    t = pl.program_id(1)
    tt = p_ref.shape[0]
    hist = CONV_WIDTH - 1
    glu = p_ref[...] * jax.nn.sigmoid(q_ref[...])
    halo = hp_ref[...] * jax.nn.sigmoid(hq_ref[...])
    buf_ref[0:CONV_HALO, :] = jnp.where(t > 0, halo, 0.0)
    buf_ref[CONV_HALO:, :] = glu
    span = rot_ref.shape[1]
    for r in range(1, SUBLANES):
        rot_ref[r - 1] = buf_ref[r:r + span, :]
    base = CONV_HALO - hist
    for k in range(CONV_WIDTH):
        wb_ref[k] = jnp.broadcast_to(dw_ref[k:k + 1, :], wb_ref.shape[1:])
    subs = range(CONV_ROWS // SUBLANES)
    lane_chunks = p_ref.shape[1] // CONV_LANES

    def chunk(i, carry):
        r0 = pl.multiple_of((i // lane_chunks) * CONV_ROWS, CONV_ROWS)
        lanes = pl.ds(pl.multiple_of((i % lane_chunks) * CONV_LANES, CONV_LANES), CONV_LANES)
        at = lambda src, off, j: src[pl.ds(off + r0 + j * SUBLANES, SUBLANES), lanes]
        bias = db_ref[:, lanes]
        w = wb_ref[hist, :, lanes]
        acc = [bias + w * at(buf_ref, CONV_HALO, j) for j in subs]
        for k in range(hist):
            r = (base + k) % SUBLANES
            a = base + k - r
            src = buf_ref if r == 0 else rot_ref.at[r - 1]
            w = wb_ref[k, :, lanes]
            acc = [acc[j] + w * at(src, a, j) for j in subs]
        for j in subs:
            y_ref[pl.ds(r0 + j * SUBLANES, SUBLANES), lanes] = acc[j]
        return carry

    lax.fori_loop(0, (tt // CONV_ROWS) * lane_chunks, chunk, 0, unroll=2)
    o_ref[...] = _silu(_layernorm_rows(y_ref[...], lg_ref[...], lb_ref[...])).astype(BF16)

    @pl.when(t == pl.num_programs(1) - 1)
    def _():
        st_ref[...] = buf_ref[CONV_HALO + tt - hist:CONV_HALO + tt, :]


def _conv_prompt(z3, lyr, n_lyr, dw, db, ln_g, ln_b, carried, tt):
    b, t_len, _ = z3.shape
    ch = db.shape[-1]
    hb = tt // CONV_HALO
    hist = CONV_WIDTH - 1
    halo = lambda c: pl.BlockSpec((None, CONV_HALO, ch), lambda i, t: (i, jnp.maximum(t * hb - 1, 0), c))
    stacked = {1: ((None, hist, ch), lambda i, t: (i, 0, 0))}
    call, extra = _stacked_call(
        _conv_prompt_kernel, 8, carried, stacked, lyr,
        grid=(b, t_len // tt),
        in_specs=[
            pl.BlockSpec((None, tt, ch), lambda i, t: (i, t, 0)),
            pl.BlockSpec((None, tt, ch), lambda i, t: (i, t, 1)),
            halo(0), halo(1),
            _whole(lyr, dw), _whole(lyr, db), _whole(lyr, ln_g), _whole(lyr, ln_b),
        ],
        out_specs=[
            pl.BlockSpec((None, tt, ch), lambda i, t: (i, t, 0)),
            None,
        ],
        out_shape=[jax.ShapeDtypeStruct((b, t_len, ch), BF16), jax.ShapeDtypeStruct((n_lyr, b, hist, ch), F32)],
        scratch_shapes=[pltpu.VMEM((CONV_HALO + tt, ch), F32),
                        pltpu.VMEM((SUBLANES - 1, CONV_HALO + tt - SUBLANES, ch), F32),
                        pltpu.VMEM((tt, ch), F32), pltpu.VMEM((CONV_WIDTH, SUBLANES, ch), F32)],
        compiler_params=_params("parallel", "arbitrary"),
        name="conv_prompt",
    )
    return call(z3, z3, z3, z3, dw, db, ln_g, ln_b, *extra)


def _l2norm_heads(x, scale):
    outs = []
    for h in range(x.shape[1] // DN_DK):
        xh = x[:, h * DN_DK:(h + 1) * DN_DK]
        outs.append(xh * (lax.rsqrt(jnp.sum(xh * xh, -1, keepdims=True) + EPS) * scale))
    return jnp.concatenate(outs, -1)


def _dn_short_conv(raw_refs, rows, w_ref, first, tail_ref, buf_ref, out_ref):
    hist = DN_CONV - 1
    base = DN_HALO - hist
    tt = rows.stop - rows.start
    for part, (x_ref, scale) in enumerate(zip(raw_refs, (DN_DK ** -0.5, 1.0, None))):
        ch = x_ref.shape[1]
        cols = slice(part * ch, (part + 1) * ch)
        x = x_ref[rows, :]
        buf_ref[0:DN_HALO, :] = jnp.where(first, 0.0, tail_ref[part])
        buf_ref[DN_HALO:, :] = x
        tail_ref[part] = x[tt - DN_HALO:, :]
        acc = w_ref[hist:hist + 1, cols] * x
        for k in range(hist):
            acc = acc + w_ref[k:k + 1, cols] * buf_ref[base + k:base + k + tt, :]
        y = _silu(acc)
        out_ref[part, rows, :] = y if scale is None else _l2norm_heads(y, scale)


def _gate_terms(ab, alog, dtb):
    g = -jnp.exp(alog) * jax.nn.softplus(ab + dtb)
    return g, jax.nn.sigmoid(ab)


def _each(fn, *lists):
    return [fn(*xs) for xs in zip(*lists)]


def _delta_chunk_terms(q, k, v, gi, gj, bi, masks):
    lower, strict, same_block = masks
    n = q[0].shape[0]
    decay = _each(lambda a, b: jnp.where(lower, jnp.exp(jnp.where(lower, a - b, 0.0)), 0.0), gi, gj)
    eg = _each(jnp.exp, gi)
    kb = _each(jnp.multiply, k, bi)
    kk = _each(_bdot_nt, kb, k)
    amat = _each(lambda a, d: jnp.where(strict, a * d, 0.0), kk, decay)
    ad = _each(lambda a: jnp.where(same_block, a, 0.0), amat)
    an = _each(jnp.subtract, amat, ad)
    doff = _each(jnp.negative, ad)
    pw = ad
    for _ in range(DN_SOLVE_BLOCK.bit_length() - 2):
        pw = _each(_bdot, pw, pw)
        dp = _each(_bdot, doff, pw)
        doff = _each(lambda d, p, x: d + p + x, doff, pw, dp)
    assert n // DN_SOLVE_BLOCK == 4
    bm = _each(jnp.add, an, _each(_bdot, doff, an))
    rhs = _each(lambda vv, b, kk_, e: jnp.concatenate([vv * b, kk_ * e], -1), v, bi, kb, eg)
    sol = _each(jnp.add, rhs, _each(_bdot, doff, rhs))
    b2 = _each(_bdot, bm, bm)
    sol = _each(jnp.add, sol, _each(_bdot, b2, sol))
    sol = _each(jnp.subtract, sol, _each(_bdot, bm, sol))
    qk = _each(jnp.multiply, _each(_bdot_nt, q, k), decay)
    g_last = _each(lambda a: a[n - 1:n, :], gi)
    return (_each(lambda x: x[:, :DN_DK], sol), _each(lambda x: x[:, DN_DK:], sol), qk, _each(jnp.multiply, q, eg),
            _each(lambda kk_, gl, a: kk_ * jnp.exp(gl - a), k, g_last, gi), _each(jnp.exp, g_last))


def _delta_prompt_kernel(q_raw, k_raw, v_raw, zg_ref, ab_ref, cw_ref, alog_ref, dtb_ref, ng_ref, o_ref, sfin_ref,
                         s_ref, tail_ref, buf_ref, qkv_ref):
    c = pl.program_id(1)
    n = DN_CHUNK
    heads = range(DN_HEADS)
    chunks = range(q_raw.shape[0] // n)

    @pl.when(c == 0)
    def _():
        s_ref[...] = jnp.zeros_like(s_ref)

    groups = [chunks[g:g + DN_PREP_CHUNKS] for g in range(0, len(chunks), DN_PREP_CHUNKS)]
    for g, group in enumerate(groups):
        span = slice(group[0] * n, (group[-1] + 1) * n)
        _dn_short_conv((q_raw, k_raw, v_raw), span, cw_ref, (c == 0) if g == 0 else False, tail_ref, buf_ref,
                       qkv_ref)
    q_ref, k_ref, v_ref = qkv_ref.at[0], qkv_ref.at[1], qkv_ref.at[2]

    row = lax.broadcasted_iota(jnp.int32, (n, n), 0)
    col = lax.broadcasted_iota(jnp.int32, (n, n), 1)
    lower = row >= col
    masks = (lower, row > col, (row // DN_SOLVE_BLOCK) == (col // DN_SOLVE_BLOCK))
    tri = lower.astype(F32)
    rows = [slice(j * n, (j + 1) * n) for j in chunks]
    cols = [slice(h * DN_DK, (h + 1) * DN_DK) for h in heads]
    gate = [_gate_terms(ab_ref[r, :], alog_ref[...], dtb_ref[...]) for r in rows]
    gcol = [_dot_hi(tri, g) for g, _ in gate]
    grow = [_dot_nt_hi(g.T, tri) for g, _ in gate]
    terms = []
    for group in groups:
        units = [(j, h) for j in group for h in heads]
        terms.append(_delta_chunk_terms(
            [q_ref[rows[j], cols[h]] for j, h in units], [k_ref[rows[j], cols[h]] for j, h in units],
            [v_ref[rows[j], cols[h]] for j, h in units], [gcol[j][:, h:h + 1] for j, h in units],
            [grow[j][h:h + 1, :] for j, h in units],
            [gate[j][1][:, DN_HEADS + h:DN_HEADS + h + 1] for j, h in units], masks))
    u, w, qk, qd, kd, gl = (sum((list(t[i]) for t in terms), []) for i in range(6))
    s = [s_ref[h] for h in heads]
    for j in chunks:
        at = lambda xs: xs[j * DN_HEADS:(j + 1) * DN_HEADS]
        v_new = _each(jnp.subtract, at(u), _each(_bdot, at(w), s))
        o = _each(jnp.add, _each(_bdot, at(qd), s), _each(_bdot, at(qk), v_new))
        s = _each(lambda ss, g, x: ss * g + x, s, at(gl), _each(_bdot_tn, at(kd), v_new))
        for h in heads:
            on = o[h] * lax.rsqrt(jnp.mean(o[h] * o[h], -1, keepdims=True) + EPS) * ng_ref[...]
            o_ref[rows[j], cols[h]] = (on * _silu(zg_ref[rows[j], cols[h]])).astype(BF16)
    for h in heads:
        s_ref[h] = s[h]

    @pl.when(c == pl.num_programs(1) - 1)
    def _():
        sfin_ref[...] = s_ref[...]


def _delta_prompt(z3, qkv_col, zg_col, ab3, lyr, n_lyr, conv_w, a_log, dt_bias, norm_g, carried):
    b, t_len, _ = z3.shape
    ch = DN_HEADS * DN_DK
    n = DN_CHUNK * DN_STEP_CHUNKS
    zcol = lambda col: pl.BlockSpec((None, n, ch), lambda i, c: (i, c, col))
    stacked = {1: ((None, DN_HEADS, DN_DK, DN_DK), lambda i, c: (i, 0, 0, 0))}
    call, extra = _stacked_call(
        _delta_prompt_kernel, 9, carried, stacked, lyr,
        grid=(b, t_len // n),
        in_specs=[
            zcol(qkv_col), zcol(qkv_col + 1), zcol(qkv_col + 2), zcol(zg_col),
            pl.BlockSpec((None, n, ab3.shape[-1]), lambda i, c: (i, c, 0)),
            _whole(lyr, conv_w), _whole(lyr, a_log), _whole(lyr, dt_bias), _whole(lyr, norm_g),
        ],
        out_specs=[
            pl.BlockSpec((None, n, ch), lambda i, c: (i, c, 0)),
            None,
        ],
        out_shape=[jax.ShapeDtypeStruct((b, t_len, ch), BF16),
                   jax.ShapeDtypeStruct((n_lyr, b, DN_HEADS, DN_DK, DN_DK), F32)],
        scratch_shapes=[pltpu.VMEM((DN_HEADS, DN_DK, DN_DK), F32), pltpu.VMEM((3, DN_HALO, ch), F32),
                        pltpu.VMEM((DN_HALO + DN_CHUNK * DN_PREP_CHUNKS, ch), F32), pltpu.VMEM((3, n, ch), F32)],
        compiler_params=_params("parallel", "arbitrary"),
        name="delta_prompt",
    )
    return call(z3, z3, z3, z3, ab3, conv_w, a_log, dt_bias, norm_g, *extra)


def _odd_sample_kernel(p_ref, gt_ref, q_ref, k_ref, v_ref, zg_ref, ab_ref, cst_ref, dst_ref, s_ref,
                       dw_ref, db_ref, lg_ref, lb_ref, cw_ref, alog_ref, dtb_ref, ng_ref,
                       oc_ref, od_ref, ncst_ref, ndst_ref, ns_ref, obuf_ref):
    bt, ch = p_ref.shape
    hist = CONV_WIDTH - 1
    glu = p_ref[...] * jax.nn.sigmoid(gt_ref[...])
    acc = db_ref[...] + dw_ref[hist:hist + 1, :] * glu
    for k in range(hist):
        acc = acc + dw_ref[k:k + 1, :] * cst_ref[k]
    oc_ref[...] = _silu(_layernorm_rows(acc, lg_ref[...], lb_ref[...])).astype(BF16)
    ncst_ref[0:hist - 1] = cst_ref[1:hist]
    ncst_ref[hist - 1] = glu

    dh = DN_CONV - 1
    parts = []
    for pi, x_ref in enumerate((q_ref, k_ref, v_ref)):
        cols = slice(pi * ch, (pi + 1) * ch)
        x = x_ref[...]
        y = cw_ref[dh:dh + 1, cols] * x
        for k in range(dh):
            y = y + cw_ref[k:k + 1, cols] * dst_ref[k, :, cols]
        parts.append(_silu(y))
        ndst_ref[0:dh - 1, :, cols] = dst_ref[1:dh, :, cols]
        ndst_ref[dh - 1, :, cols] = x
    q = _l2norm_heads(parts[0], DN_DK ** -0.5)
    k = _l2norm_heads(parts[1], 1.0)
    v = parts[2]
    g, beta = _gate_terms(ab_ref[...], alog_ref[...], dtb_ref[...])
    eg_all = jnp.exp(g)

    assert 2 * DN_HEADS * bt == LANES
    kq = jnp.concatenate([x[:, h * DN_DK:(h + 1) * DN_DK] for x in (k, q) for h in range(DN_HEADS)], 0)
    kq_t = kq.T
    for h in range(DN_HEADS):
        cols = slice(h * DN_DK, (h + 1) * DN_DK)
        for b in range(bt):
            kc = kq_t[:, h * bt + b:h * bt + b + 1]
            qc = kq_t[:, (DN_HEADS + h) * bt + b:(DN_HEADS + h) * bt + b + 1]
            s = s_ref[b, h]
            sk = jnp.sum(s * kc, 0, keepdims=True)
            sq = jnp.sum(s * qc, 0, keepdims=True)
            eg = eg_all[b:b + 1, h:h + 1]
            bi = beta[b:b + 1, DN_HEADS + h:DN_HEADS + h + 1]
            v_new = bi * (v[b:b + 1, cols] - eg * sk)
            qk = jnp.sum(q[b:b + 1, cols] * k[b:b + 1, cols], -1, keepdims=True)
            obuf_ref[b:b + 1, cols] = eg * sq + qk * v_new
            ns_ref[b, h] = s * eg + kc * v_new
    for h in range(DN_HEADS):
        cols = slice(h * DN_DK, (h + 1) * DN_DK)
        o = obuf_ref[:, cols]
        o = o * lax.rsqrt(jnp.mean(o * o, -1, keepdims=True) + EPS) * ng_ref[...]
        od_ref[:, cols] = (o * _silu(zg_ref[:, cols])).astype(BF16)


def _odd_sample(z, ab, lyr, cstate, dstate, sstate, dw, db, ln_g, ln_b, conv_w, a_log, dt_bias, norm_g, carried, bt):
    n = z.shape[0]
    ch = db.shape[-1]
    zcol = lambda c: pl.BlockSpec((bt, ch), lambda i: (i, c))
    lead_blk = lambda a: ((bt,) + a.shape[2:], lambda i: (i,) + (0,) * (a.ndim - 2))
    hist_blk = lambda a: ((a.shape[1], bt, a.shape[3]), lambda i: (0, i, 0))
    lead = lambda a: _layer(lyr, *lead_blk(a))
    hist = lambda a: _layer(lyr, *hist_blk(a))
    act = jax.ShapeDtypeStruct((n, ch), BF16)
    same = lambda a: jax.ShapeDtypeStruct(a.shape, F32)
    stacked = {2: hist_blk(cstate), 3: hist_blk(dstate), 4: lead_blk(sstate)}
    call, extra = _stacked_call(
        _odd_sample_kernel, 18, carried, stacked, lyr,
        grid=(n // bt,),
        in_specs=[zcol(0), zcol(1), zcol(2), zcol(3), zcol(4), zcol(5),
                  pl.BlockSpec((bt, ab.shape[-1]), lambda i: (i, 0)),
                  hist(cstate), hist(dstate), lead(sstate),
                  _whole(lyr, dw), _whole(lyr, db), _whole(lyr, ln_g), _whole(lyr, ln_b),
                  _whole(lyr, conv_w), _whole(lyr, a_log), _whole(lyr, dt_bias), _whole(lyr, norm_g)],
        out_specs=[zcol(0), zcol(0), None, None, None],
        out_shape=[act, act, same(cstate), same(dstate), same(sstate)],
        scratch_shapes=[pltpu.VMEM((bt, ch), F32)],
        compiler_params=_params("parallel"),
        name="odd_sample",
    )
    return call(z, z, z, z, z, z, ab, cstate, dstate, sstate, dw, db, ln_g, ln_b, conv_w, a_log, dt_bias, norm_g,
                *extra)


def kernel(x_prompt, x_sample, state_pool, state_conv_c, state_dn_conv, state_dn_S, norm_mix, norm_ffn, norm_final, ev_w_in, pool_w, pool_scale, sgu_ln_g, sgu_ln_b, sgu_ws, sgu_b, ev_w_out, od_w_in, cv_dw, cv_db, cv_ln_g, cv_ln_b, dn_conv_w, dn_a_log, dn_dt_bias, dn_norm_g, od_w_out, ffn_w_up, ffn_w_down):
    bp, t_len, d = x_prompt.shape
    bs = x_sample.shape[0]
    assert x_sample.shape[1] == 1
    depth = norm_mix.shape[0]
    n_even, n_odd = ev_w_in.shape[0], od_w_in.shape[0]
    ch = pool_scale.shape[1]
    odd_main = 6 * ch
    n_gate = 2 * DN_HEADS
    assert od_w_in.shape[2] == odd_main + n_gate and odd_main % n_gate == 0

    rows = lambda a: a.reshape(a.shape[0], 1, a.shape[-1])
    gate_rows = lambda a: rows(jnp.pad(a, ((0, 0), (0, DN_HEADS))))
    od_in = jnp.swapaxes(od_w_in, 1, 2)
    pool_wb = pool_w.astype(BF16)
    g_mix, g_ffn, g_fin = rows(norm_mix), rows(norm_ffn), norm_final.reshape(1, d)
    p_scale, s_lg, s_lb = rows(pool_scale), rows(sgu_ln_g), rows(sgu_ln_b)
    s_bt = jnp.swapaxes(sgu_b, 1, 2)
    s_w0 = rows(jnp.repeat(sgu_ws[:, :, 0, 0], ch // SGU_HEADS, axis=1))
    s_b0 = rows(jnp.repeat(sgu_b[:, :, 0], ch // SGU_HEADS, axis=1))
    c_db, c_lg, c_lb = rows(cv_db), rows(cv_ln_g), rows(cv_ln_b)
    a_log, dt_bias, d_ng = gate_rows(dn_a_log), gate_rows(dn_dt_bias), rows(dn_norm_g)

    hist_major = lambda a: jnp.swapaxes(a, 1, 2)
    st_pool, st_cc, st_dc = hist_major(state_pool), hist_major(state_conv_c), hist_major(state_dn_conv)

    xp = x_prompt.reshape(bp * t_len, d)
    xs = x_sample.reshape(bs, d)
    tn = 2 * _tiles(bp * t_len)[2]
    flat = lambda a: a.reshape(bp * t_len, a.shape[-1])
    pool_p, dc_p = [], []
    even_s = cc_p = s_p = odd_s = None
    for l in range(depth):
        i = l // 2
        if l % 2 == 0:
            zs, w_in = _norm_matmul(xs, g_mix, l, ev_w_in, i, 3 * ch, tn, emit_weights=True)
            zp, = _norm_matmul(xp, g_mix, l, w_in, 0, 3 * ch, tn)
            zp3 = zp.reshape(bp, t_len, 3 * ch)
            a_s, b_s, *even_s = _even_sample(zs, i, st_pool, pool_wb, p_scale, s_lg, s_lb, s_w0, s_b0, even_s, 8)
            pool_p.append(zp3[:, t_len - POOL_HIST:, :ch])
            xs, w_out = _mix_out(a_s, b_s, ev_w_out, i, xs, emit_weights=True)
            xp = flat(_even_prompt(zp3, xp.reshape(bp, t_len, d), i, pool_wb, p_scale, s_lg, s_lb, sgu_ws, s_bt,
                                   w_out, 512))
        else:
            zs, abs_, w_in, w_gate = _norm_matmul(xs, g_mix, l, od_in, i, odd_main, tn, True,
                                                  (od_in, i, odd_main // n_gate, n_gate), emit_weights=True)
            zp, abp = _norm_matmul(xp, g_mix, l, w_in, 0, odd_main, tn, True, (w_gate, 0, 0, n_gate))
            zp3 = zp.reshape(bp, t_len, odd_main)
            c_p, *cc_p = _conv_prompt(zp3, i, n_odd, cv_dw, c_db, c_lg, c_lb, cc_p, 512)
            d_p, *s_p = _delta_prompt(zp3, 2, 5, abp.reshape(bp, t_len, n_gate), i, n_odd, dn_conv_w, a_log, dt_bias,
                                      d_ng, s_p)
            c_s, d_s, *odd_s = _odd_sample(zs, abs_, i, st_cc, st_dc, state_dn_S, cv_dw, c_db, c_lg,
                                           c_lb, dn_conv_w, a_log, dt_bias, d_ng, odd_s, 8)
            dc_p.append(zp3[:, t_len - (DN_CONV - 1):, 2 * ch:5 * ch])
            xs, w_out = _mix_out(c_s, d_s, od_w_out, i, xs, emit_weights=True)
            xp = _mix_out(flat(c_p), flat(d_p), w_out, 0, xp)
        last = l == depth - 1
        xs, *w_ffn = _ffn(xs, g_ffn, l, (ffn_w_up, ffn_w_down), g_fin, last)
        xp, = _ffn(xp, g_ffn, l, w_ffn, g_fin, last)
    pool_s, v_s = even_s
    cc_s, dc_s, s_s = odd_s
    return (xp.reshape(bp, t_len, d), xs.reshape(bs, 1, d), jnp.stack(pool_p), hist_major(pool_s),
            v_s.reshape(n_even, bs, 1, ch), cc_p[0], hist_major(cc_s), jnp.stack(dc_p), hist_major(dc_s), s_p[0], s_s)
```

```python
import functools

import jax
import jax.numpy as jnp
from jax import lax
from jax.experimental import pallas as pl
from jax.experimental.pallas import tpu as pltpu

F32 = jnp.float32
BF16 = jnp.bfloat16
HI = lax.Precision.HIGHEST

EPS = 1e-6
PAST_LEN = 16384
POOL_WINDOWS = (2, 4, 8, 16)
POOL_HIST = max(POOL_WINDOWS) - 1
SGU_HEADS = 4
SGU_CHUNK = 128
CONV_WIDTH = 31
DN_HEADS = 8
DN_DK = 128
DN_CONV = 4
DN_CHUNK = 64
DN_SOLVE_BLOCK = 16
DN_STEP_CHUNKS = 4
DN_PREP_CHUNKS = 2

LANES = 128
SUBLANES = 8
VMEM_LIMIT = 56 * 1024 * 1024

POOL_HALO = 16
CONV_HALO = 32
DN_HALO = 8
CONV_ROWS, CONV_LANES = 64, 256


def _tiles(m):
    return min(m, 1024), min(m, 512), 512


def _params(*semantics):
    return pltpu.CompilerParams(dimension_semantics=semantics, vmem_limit_bytes=VMEM_LIMIT)


def _layer(lyr, block, index_map):
    return pl.BlockSpec((None,) + tuple(block), lambda *g: (lyr,) + tuple(index_map(*g)))


def _whole(lyr, arr):
    return pl.BlockSpec((None,) + arr.shape[1:], lambda *g: (lyr,) + (0,) * (arr.ndim - 1))


def _skip_refs(fn, start, count):
    def body(*refs):
        return fn(*refs[:start], *refs[start + count:])
    return body


def _own_layer_zero_rest(fn, n_in, outs, lyr):
    def body(*refs):
        refs = list(refs)
        for o in outs:
            full = refs[n_in + o]
            for other in range(full.shape[0]):
                if other != lyr:
                    full[other] = jnp.zeros(full.shape[1:], full.dtype)
            refs[n_in + o] = full.at[lyr]
        return fn(*refs)
    return body


def _stacked_call(kernel_fn, n_in, carried, stacked_out, lyr, **kw):
    out_specs = list(kw.pop("out_specs"))
    if carried is None:
        for o, (block, imap) in stacked_out.items():
            n_lyr = kw["out_shape"][o].shape[0]
            out_specs[o] = pl.BlockSpec((n_lyr,) + tuple(block), lambda *g, imap=imap: (0,) + tuple(imap(*g)))
        body = _own_layer_zero_rest(kernel_fn, n_in, tuple(stacked_out), lyr)
        return pl.pallas_call(body, out_specs=out_specs, **kw), ()
    for o, (block, imap) in stacked_out.items():
        out_specs[o] = _layer(lyr, block, imap)
    kw["in_specs"] = list(kw["in_specs"]) + [pl.BlockSpec(memory_space=pl.ANY)] * len(carried)
    aliases = {n_in + k: o for k, o in enumerate(stacked_out)}
    body = _skip_refs(kernel_fn, n_in, len(carried))
    return pl.pallas_call(body, out_specs=out_specs, input_output_aliases=aliases, **kw), tuple(carried)


def _rms_rows(x, g):
    return x * lax.rsqrt(jnp.mean(x * x, -1, keepdims=True) + EPS) * g


def _layernorm_rows(x, g, b):
    mu = jnp.mean(x, -1, keepdims=True)
    xc = x - mu
    var = jnp.mean(xc * xc, -1, keepdims=True)
    return xc * lax.rsqrt(var + EPS) * g + b


def _gelu(x):
    return 0.5 * x * (1.0 + lax.erf(x * 0.7071067811865476))


def _silu(x):
    return x * jax.nn.sigmoid(x)


def _dot(a, b):
    return jnp.dot(a, b, preferred_element_type=F32)


def _dot_hi(a, b):
    return jnp.dot(a, b, precision=HI, preferred_element_type=F32)


def _dot_nt_hi(a, b):
    return lax.dot_general(a, b, (((1,), (1,)), ((), ())), precision=HI, preferred_element_type=F32)


def _bdot(a, b):
    return jnp.dot(a.astype(BF16), b.astype(BF16), preferred_element_type=F32)


def _bdot_nt(a, b):
    return lax.dot_general(a.astype(BF16), b.astype(BF16), (((1,), (1,)), ((), ())), preferred_element_type=F32)


def _bdot_tn(a, b):
    return lax.dot_general(a.astype(BF16), b.astype(BF16), (((0,), (0,)), ((), ())), preferred_element_type=F32)


def _dot_nt(a, b):
    return lax.dot_general(a, b, (((1,), (1,)), ((), ())), preferred_element_type=F32)


def _norm_matmul_kernel(x_ref, g_ref, w_ref, *rest, w_rows_out, names):
    r = dict(zip(names, rest))
    o_ref, h_ref = r["out"], r["h"]

    @pl.when(pl.program_id(1) == 0)
    def _():
        h = _rms_rows(x_ref[...], g_ref[...]).astype(BF16)
        h_ref[...] = h
        if "w_tail" in r:
            wt = r["w_tail"][...].astype(BF16)
            r["out_tail"][...] = _dot_nt(h, wt)
            if "emit_tail" in r:
                r["emit_tail"][...] = wt

    w = w_ref[...].astype(BF16)
    if "emit" in r:
        r["emit"][...] = w
    o_ref[...] = _dot_nt(h_ref[...], w) if w_rows_out else _dot(h_ref[...], w)


def _norm_matmul(x, gains, lyr, w, wl, n_out, tn, w_rows_out=False, tail=None, emit_weights=False):
    m, d = x.shape
    tm = _tiles(m)[0]
    if w_rows_out:
        w_spec = _layer(wl, (tn, d), lambda i, j: (j, 0))
    else:
        assert tail is None
        w_spec = _layer(wl, (d, tn), lambda i, j: (0, j))
    in_specs = [pl.BlockSpec((tm, d), lambda i, j: (i, 0)), _whole(lyr, gains), w_spec]
    out_specs = [pl.BlockSpec((tm, tn), lambda i, j: (i, j))]
    out_shape = [jax.ShapeDtypeStruct((m, n_out), F32)]
    operands = [x, gains, w]
    names = ["out"]
    if tail is not None:
        tail_w, tail_l, tail_block, n_tail = tail
        in_specs.append(_layer(tail_l, (n_tail, d), lambda i, j: (tail_block, 0)))
        operands.append(tail_w)
        out_specs.append(pl.BlockSpec((tm, n_tail), lambda i, j: (i, 0)))
        out_shape.append(jax.ShapeDtypeStruct((m, n_tail), F32))
        names = ["w_tail", "out", "out_tail"]
    if emit_weights:
        assert m == tm
        w_block = (None, tn, d) if w_rows_out else (None, d, tn)
        out_specs.append(pl.BlockSpec(w_block, (lambda i, j: (0, j, 0)) if w_rows_out else (lambda i, j: (0, 0, j))))
        out_shape.append(jax.ShapeDtypeStruct((1, n_out, d) if w_rows_out else (1, d, n_out), BF16))
        names.append("emit")
        if tail is not None:
            out_specs.append(pl.BlockSpec((None, n_tail, d), lambda i, j: (0, 0, 0)))
            out_shape.append(jax.ShapeDtypeStruct((1, n_tail, d), BF16))
            names.append("emit_tail")
    names.append("h")
    return pl.pallas_call(
        functools.partial(_norm_matmul_kernel, w_rows_out=w_rows_out, names=tuple(names)),
        grid=(m // tm, n_out // tn),
        in_specs=in_specs,
        out_specs=out_specs,
        out_shape=out_shape,
        scratch_shapes=[pltpu.VMEM((tm, d), BF16)],
        compiler_params=_params("parallel", "arbitrary"),
        name="norm_matmul",
    )(*operands)


def _mix_out_kernel(a0_ref, a1_ref, w0_ref, w1_ref, x_ref, o_ref, *w_out):
    w0, w1 = w0_ref[...].astype(BF16), w1_ref[...].astype(BF16)
    if w_out:
        w_out[0][0], w_out[0][1] = w0, w1
    o_ref[...] = x_ref[...] + _dot(a0_ref[...], w0) + _dot(a1_ref[...], w1)


def _mix_out(a0, a1, w, wl, x, emit_weights=False):
    m, kh = a0.shape
    n = w.shape[2]
    tm = _tiles(m)[1]
    assert not emit_weights or m == tm
    w_half = lambda r: pl.BlockSpec((None, kh, n), lambda i: (wl, r, 0), pipeline_mode=pl.Buffered(1))
    rows = pl.BlockSpec((tm, n), lambda i: (i, 0))
    out = pl.pallas_call(
        _mix_out_kernel,
        grid=(m // tm,),
        in_specs=[
            pl.BlockSpec((tm, kh), lambda i: (i, 0)),
            pl.BlockSpec((tm, kh), lambda i: (i, 0)),
            w_half(0), w_half(1),
            rows,
        ],
        out_specs=[rows] + [pl.BlockSpec((2, kh, n), lambda i: (0, 0, 0))] * emit_weights,
        out_shape=[jax.ShapeDtypeStruct((m, n), F32)] + [jax.ShapeDtypeStruct((2, kh, n), BF16)] * emit_weights,
        compiler_params=_params("parallel"),
        name="mix_out",
    )(a0, a1, w, w, x)
    return (out[0], out[1].reshape(1, 2 * kh, n)) if emit_weights else out[0]


def _ffn_kernel(x_ref, g_ref, wg_ref, wu_ref, wd_ref, gf_ref, o_ref, *rest, final_norm, emit_weights):
    h_ref = rest[-1]
    f = pl.program_id(1)

    @pl.when(f == 0)
    def _():
        x = x_ref[...]
        h_ref[...] = _rms_rows(x, g_ref[...]).astype(BF16)
        o_ref[...] = x

    wg, wu, wd = wg_ref[...].astype(BF16), wu_ref[...].astype(BF16), wd_ref[...].astype(BF16)
    if emit_weights:
        wg_out, wu_out, wd_out = rest[:3]
        wg_out[...] = wg
        wu_out[...] = wu
        wd_out[...] = wd
    h = h_ref[...]
    act = _silu(_dot(h, wg)) * _dot(h, wu)
    o_ref[...] += _dot(act.astype(BF16), wd)

    if final_norm:
        @pl.when(f == pl.num_programs(1) - 1)
        def _():
            o_ref[...] = _rms_rows(o_ref[...], gf_ref[...])


def _ffn(x, gains, lyr, weights, g_final, final_norm):
    m, d = x.shape
    _, tm, tf = _tiles(m)
    emit_weights = False
    if len(weights) == 2:
        tm, tf = min(m, 1024), tf // 2
        w_up, w_down = weights
        d_ff = w_down.shape[1]
        nf = d_ff // tf
        operands = (w_up, w_up, w_down)
        w_specs = [_layer(lyr, (d, tf), lambda i, f: (0, f)), _layer(lyr, (d, tf), lambda i, f: (0, f + nf)),
                   _layer(lyr, (tf, d), lambda i, f: (f, 0))]
        out_specs, out_shape = [], []
    else:
        operands = weights
        d_ff = weights[2].shape[0]
        nf = d_ff // tf
        w_specs = [pl.BlockSpec((d, tf), lambda i, f: (0, f)), pl.BlockSpec((d, tf), lambda i, f: (0, f)),
                   pl.BlockSpec((tf, d), lambda i, f: (f, 0))]
        out_specs, out_shape = [], []
    return pl.pallas_call(
        functools.partial(_ffn_kernel, final_norm=final_norm, emit_weights=emit_weights),
        grid=(m // tm, nf),
        in_specs=[pl.BlockSpec((tm, d), lambda i, f: (i, 0)), _whole(lyr, gains), *w_specs,
                  pl.BlockSpec((1, d), lambda i, f: (0, 0))],
        out_specs=[pl.BlockSpec((tm, d), lambda i, f: (i, 0)), *out_specs],
        out_shape=[jax.ShapeDtypeStruct((m, d), F32), *out_shape],
        scratch_shapes=[pltpu.VMEM((tm, d), BF16)],
        compiler_params=_params("parallel", "arbitrary"),
        name="ffn",
    )(x, gains, *operands, g_final)


def _even_prompt_kernel(a_ref, halo_ref, u_ref, v_ref, pw_ref, ps_ref, lg_ref, lb_ref, ws_ref, bst_ref,
                        x_ref, w0_ref, w1_ref, o_ref, buf_ref, oa_ref, ob_ref):
    t = pl.program_id(1)
    tt, ch = a_ref.shape
    gch = ch // len(POOL_WINDOWS)
    hch = ch // SGU_HEADS
    buf_ref[0:POOL_HALO, :] = jnp.where(t > 0, halo_ref[...], 0.0)
    buf_ref[POOL_HALO:, :] = a_ref[...]
    row = lax.broadcasted_iota(jnp.int32, (SGU_CHUNK, SGU_CHUNK), 0)
    col = lax.broadcasted_iota(jnp.int32, (SGU_CHUNK, SGU_CHUNK), 1)
    wm = [jnp.where(row >= col, ws_ref[h], 0.0).astype(BF16) for h in range(SGU_HEADS)]
    for c in range(tt // SGU_CHUNK):
        rows = slice(c * SGU_CHUNK, (c + 1) * SGU_CHUNK)
        a = a_ref[rows, :]
        pos = t * tt + c * SGU_CHUNK + lax.broadcasted_iota(jnp.int32, (SGU_CHUNK, 1), 0)
        for gi, w in enumerate(POOL_WINDOWS):
            cols = slice(gi * gch, (gi + 1) * gch)
            s = a[:, cols]
            for k in range(1, w):
                start = POOL_HALO + c * SGU_CHUNK - k
                s = s + buf_ref[start:start + SGU_CHUNK, cols]
            cnt = jnp.minimum(w, pos + 1).astype(F32)
            pooled = s / cnt - a[:, cols]
            mixed = _dot(pooled.astype(BF16), pw_ref[gi])
            oa_ref[rows, cols] = (mixed * ps_ref[:, cols]).astype(BF16)

        u = _gelu(u_ref[rows, :])
        vn = _layernorm_rows(_gelu(v_ref[rows, :]), lg_ref[...], lb_ref[...])
        for h in range(SGU_HEADS):
            cols = slice(h * hch, (h + 1) * hch)
            mixed = _dot(wm[h], vn[:, cols].astype(BF16)) + bst_ref[:, h:h + 1]
            ob_ref[rows, cols] = (u[:, cols] * mixed).astype(BF16)

        o_ref[rows, :] = x_ref[rows, :] + _dot(oa_ref[rows, :], w0_ref[...]) + _dot(ob_ref[rows, :], w1_ref[...])


def _even_prompt(z3, x3, lyr, pool_w, pool_scale, ln_g, ln_b, ws, bst, w_out, tt):
    b, t_len, _ = z3.shape
    ch = pool_scale.shape[-1]
    d = x3.shape[-1]
    hb = tt // POOL_HALO
    w_half = lambda r: pl.BlockSpec((None, ch, d), lambda i, t: (0, r, 0), pipeline_mode=pl.Buffered(1))
    rows = pl.BlockSpec((None, tt, d), lambda i, t: (i, t, 0))
    return pl.pallas_call(
        _even_prompt_kernel,
        grid=(b, t_len // tt),
        in_specs=[
            pl.BlockSpec((None, tt, ch), lambda i, t: (i, t, 0)),
            pl.BlockSpec((None, POOL_HALO, ch), lambda i, t: (i, jnp.maximum(t * hb - 1, 0), 0)),
            pl.BlockSpec((None, tt, ch), lambda i, t: (i, t, 1)),
            pl.BlockSpec((None, tt, ch), lambda i, t: (i, t, 2)),
            _whole(lyr, pool_w), _whole(lyr, pool_scale), _whole(lyr, ln_g), _whole(lyr, ln_b),
            _whole(lyr, ws), _whole(lyr, bst),
            rows, w_half(0), w_half(1),
        ],
        out_specs=rows,
        out_shape=jax.ShapeDtypeStruct(x3.shape, F32),
        scratch_shapes=[pltpu.VMEM((POOL_HALO + tt, ch), F32), pltpu.VMEM((tt, ch), BF16),
                        pltpu.VMEM((tt, ch), BF16)],
        compiler_params=_params("parallel", "arbitrary"),
        name="even_prompt",
    )(z3, z3, z3, z3, pool_w, pool_scale, ln_g, ln_b, ws, bst, x3, w_out, w_out)


def _even_sample_kernel(a_ref, u_ref, v_ref, st_ref, pw_ref, ps_ref, lg_ref, lb_ref, w0_ref, b0_ref,
                        oa_ref, ob_ref, nst_ref, nv_ref):
    ch = a_ref.shape[1]
    gch = ch // len(POOL_WINDOWS)
    a = a_ref[...]
    for gi, w in enumerate(POOL_WINDOWS):
        cols = slice(gi * gch, (gi + 1) * gch)
        s = a[:, cols]
        for k in range(1, w):
            s = s + st_ref[POOL_HIST - k, :, cols]
        pooled = s / float(min(w, PAST_LEN + 1)) - a[:, cols]
        mixed = _dot(pooled.astype(BF16), pw_ref[gi])
        oa_ref[:, cols] = (mixed * ps_ref[:, cols]).astype(BF16)
    nst_ref[0:POOL_HIST - 1] = st_ref[1:POOL_HIST]
    nst_ref[POOL_HIST - 1] = a

    u = _gelu(u_ref[...])
    vn = _layernorm_rows(_gelu(v_ref[...]), lg_ref[...], lb_ref[...])
    nv_ref[...] = vn
    ob_ref[...] = (u * (vn * w0_ref[...] + b0_ref[...])).astype(BF16)


def _even_sample(z, lyr, state, pool_w, pool_scale, ln_g, ln_b, w0, b0, carried, bt):
    assert PAST_LEN % SGU_CHUNK == 0
    n = z.shape[0]
    ch = pool_scale.shape[-1]
    act = jax.ShapeDtypeStruct((n, ch), BF16)
    rows = pl.BlockSpec((bt, ch), lambda i: (i, 0))
    n_in = 10
    stacked = {2: ((POOL_HIST, bt, ch), lambda i: (0, i, 0)), 3: ((bt, ch), lambda i: (i, 0))}
    call, extra = _stacked_call(
        _even_sample_kernel, n_in, carried, stacked, lyr,
        grid=(n // bt,),
        in_specs=[
            rows, pl.BlockSpec((bt, ch), lambda i: (i, 1)), pl.BlockSpec((bt, ch), lambda i: (i, 2)),
            _layer(lyr, (POOL_HIST, bt, ch), lambda i: (0, i, 0)),
            _whole(lyr, pool_w), _whole(lyr, pool_scale), _whole(lyr, ln_g), _whole(lyr, ln_b),
            _whole(lyr, w0), _whole(lyr, b0),
        ],
        out_specs=[rows, rows, None, None],
        out_shape=[act, act, jax.ShapeDtypeStruct(state.shape, F32),
                   jax.ShapeDtypeStruct((state.shape[0], n, ch), F32)],
        compiler_params=_params("parallel"),
        name="even_sample",
    )
    return call(z, z, z, state, pool_w, pool_scale, ln_g, ln_b, w0, b0, *extra)


def _conv_prompt_kernel(p_ref, q_ref, hp_ref, hq_ref, dw_ref, db_ref, lg_ref, lb_ref, o_ref, st_ref, buf_ref,
                        rot_ref, y_ref, wb_ref):
    t = pl.program_id(1)
    tt = p_ref.shape[0]
    hist = CONV_WIDTH - 1
    glu = p_ref[...] * jax.nn.sigmoid(q_ref[...])
    halo = hp_ref[...] * jax.nn.sigmoid(hq_ref[...])
    buf_ref[0:CONV_HALO, :] = jnp.where(t > 0, halo, 0.0)
    buf_ref[CONV_HALO:, :] = glu
    span = rot_ref.shape[1]
    for r in range(1, SUBLANES):
        rot_ref[r - 1] = buf_ref[r:r + span, :]
    base = CONV_HALO - hist
    for k in range(CONV_WIDTH):
        wb_ref[k] = jnp.broadcast_to(dw_ref[k:k + 1, :], wb_ref.shape[1:])
    subs = range(CONV_ROWS // SUBLANES)
    lane_chunks = p_ref.shape[1] // CONV_LANES

    def chunk(i, carry):
        r0 = pl.multiple_of((i // lane_chunks) * CONV_ROWS, CONV_ROWS)
        lanes = pl.ds(pl.multiple_of((i % lane_chunks) * CONV_LANES, CONV_LANES), CONV_LANES)
        at = lambda src, off, j: src[pl.ds(off + r0 + j * SUBLANES, SUBLANES), lanes]
        bias = db_ref[:, lanes]
        w = wb_ref[hist, :, lanes]
        acc = [bias + w * at(buf_ref, CONV_HALO, j) for j in subs]
        for k in range(hist):
            r = (base + k) % SUBLANES
            a = base + k - r
            src = buf_ref if r == 0 else rot_ref.at[r - 1]
            w = wb_ref[k, :, lanes]
            acc = [acc[j] + w * at(src, a, j) for j in subs]
        for j in subs:
            y_ref[pl.ds(r0 + j * SUBLANES, SUBLANES), lanes] = acc[j]
        return carry

    lax.fori_loop(0, (tt // CONV_ROWS) * lane_chunks, chunk, 0, unroll=2)
    o_ref[...] = _silu(_layernorm_rows(y_ref[...], lg_ref[...], lb_ref[...])).astype(BF16)

    @pl.when(t == pl.num_programs(1) - 1)
    def _():
        st_ref[...] = buf_ref[CONV_HALO + tt - hist:CONV_HALO + tt, :]


def _conv_prompt(z3, lyr, n_lyr, dw, db, ln_g, ln_b, carried, tt):
    b, t_len, _ = z3.shape
    ch = db.shape[-1]
    hb = tt // CONV_HALO
    hist = CONV_WIDTH - 1
    halo = lambda c: pl.BlockSpec((None, CONV_HALO, ch), lambda i, t: (i, jnp.maximum(t * hb - 1, 0), c))
    stacked = {1: ((None, hist, ch), lambda i, t: (i, 0, 0))}
    call, extra = _stacked_call(
        _conv_prompt_kernel, 8, carried, stacked, lyr,
        grid=(b, t_len // tt),
        in_specs=[
            pl.BlockSpec((None, tt, ch), lambda i, t: (i, t, 0)),
            pl.BlockSpec((None, tt, ch), lambda i, t: (i, t, 1)),
            halo(0), halo(1),
            _whole(lyr, dw), _whole(lyr, db), _whole(lyr, ln_g), _whole(lyr, ln_b),
        ],
        out_specs=[
            pl.BlockSpec((None, tt, ch), lambda i, t: (i, t, 0)),
            None,
        ],
        out_shape=[jax.ShapeDtypeStruct((b, t_len, ch), BF16), jax.ShapeDtypeStruct((n_lyr, b, hist, ch), F32)],
        scratch_shapes=[pltpu.VMEM((CONV_HALO + tt, ch), F32),
                        pltpu.VMEM((SUBLANES - 1, CONV_HALO + tt - SUBLANES, ch), F32),
                        pltpu.VMEM((tt, ch), F32), pltpu.VMEM((CONV_WIDTH, SUBLANES, ch), F32)],
        compiler_params=_params("parallel", "arbitrary"),
        name="conv_prompt",
    )
    return call(z3, z3, z3, z3, dw, db, ln_g, ln_b, *extra)


def _l2norm_heads(x, scale):
    outs = []
    for h in range(x.shape[1] // DN_DK):
        xh = x[:, h * DN_DK:(h + 1) * DN_DK]
        outs.append(xh * (lax.rsqrt(jnp.sum(xh * xh, -1, keepdims=True) + EPS) * scale))
    return jnp.concatenate(outs, -1)


def _dn_short_conv(raw_refs, rows, w_ref, first, tail_ref, buf_ref, out_ref):
    hist = DN_CONV - 1
    base = DN_HALO - hist
    tt = rows.stop - rows.start
    for part, (x_ref, scale) in enumerate(zip(raw_refs, (DN_DK ** -0.5, 1.0, None))):
        ch = x_ref.shape[1]
        cols = slice(part * ch, (part + 1) * ch)
        x = x_ref[rows, :]
        buf_ref[0:DN_HALO, :] = jnp.where(first, 0.0, tail_ref[part])
        buf_ref[DN_HALO:, :] = x
        tail_ref[part] = x[tt - DN_HALO:, :]
        acc = w_ref[hist:hist + 1, cols] * x
        for k in range(hist):
            acc = acc + w_ref[k:k + 1, cols] * buf_ref[base + k:base + k + tt, :]
        y = _silu(acc)
        out_ref[part, rows, :] = y if scale is None else _l2norm_heads(y, scale)


def _gate_terms(ab, alog, dtb):
    g = -jnp.exp(alog) * jax.nn.softplus(ab + dtb)
    return g, jax.nn.sigmoid(ab)


def _each(fn, *lists):
    return [fn(*xs) for xs in zip(*lists)]


def _delta_chunk_terms(q, k, v, gi, gj, bi, masks):
    lower, strict, same_block = masks
    n = q[0].shape[0]
    decay = _each(lambda a, b: jnp.where(lower, jnp.exp(jnp.where(lower, a - b, 0.0)), 0.0), gi, gj)
    eg = _each(jnp.exp, gi)
    kb = _each(jnp.multiply, k, bi)
    kk = _each(_bdot_nt, kb, k)
    amat = _each(lambda a, d: jnp.where(strict, a * d, 0.0), kk, decay)
    ad = _each(lambda a: jnp.where(same_block, a, 0.0), amat)
    an = _each(jnp.subtract, amat, ad)
    doff = _each(jnp.negative, ad)
    pw = ad
    for _ in range(DN_SOLVE_BLOCK.bit_length() - 2):
        pw = _each(_bdot, pw, pw)
        dp = _each(_bdot, doff, pw)
        doff = _each(lambda d, p, x: d + p + x, doff, pw, dp)
    assert n // DN_SOLVE_BLOCK == 4
    bm = _each(jnp.add, an, _each(_bdot, doff, an))
    rhs = _each(lambda vv, b, kk_, e: jnp.concatenate([vv * b, kk_ * e], -1), v, bi, kb, eg)
    sol = _each(jnp.add, rhs, _each(_bdot, doff, rhs))
    b2 = _each(_bdot, bm, bm)
    sol = _each(jnp.add, sol, _each(_bdot, b2, sol))
    sol = _each(jnp.subtract, sol, _each(_bdot, bm, sol))
    qk = _each(jnp.multiply, _each(_bdot_nt, q, k), decay)
    g_last = _each(lambda a: a[n - 1:n, :], gi)
    return (_each(lambda x: x[:, :DN_DK], sol), _each(lambda x: x[:, DN_DK:], sol), qk, _each(jnp.multiply, q, eg),
            _each(lambda kk_, gl, a: kk_ * jnp.exp(gl - a), k, g_last, gi), _each(jnp.exp, g_last))


def _delta_prompt_kernel(q_raw, k_raw, v_raw, zg_ref, ab_ref, cw_ref, alog_ref, dtb_ref, ng_ref, o_ref, sfin_ref,
                         s_ref, tail_ref, buf_ref, qkv_ref):
    c = pl.program_id(1)
    n = DN_CHUNK
    heads = range(DN_HEADS)
    chunks = range(q_raw.shape[0] // n)

    @pl.when(c == 0)
    def _():
        s_ref[...] = jnp.zeros_like(s_ref)

    groups = [chunks[g:g + DN_PREP_CHUNKS] for g in range(0, len(chunks), DN_PREP_CHUNKS)]
    for g, group in enumerate(groups):
        span = slice(group[0] * n, (group[-1] + 1) * n)
        _dn_short_conv((q_raw, k_raw, v_raw), span, cw_ref, (c == 0) if g == 0 else False, tail_ref, buf_ref,
                       qkv_ref)
    q_ref, k_ref, v_ref = qkv_ref.at[0], qkv_ref.at[1], qkv_ref.at[2]

    row = lax.broadcasted_iota(jnp.int32, (n, n), 0)
    col = lax.broadcasted_iota(jnp.int32, (n, n), 1)
    lower = row >= col
    masks = (lower, row > col, (row // DN_SOLVE_BLOCK) == (col // DN_SOLVE_BLOCK))
    tri = lower.astype(F32)
    rows = [slice(j * n, (j + 1) * n) for j in chunks]
    cols = [slice(h * DN_DK, (h + 1) * DN_DK) for h in heads]
    gate = [_gate_terms(ab_ref[r, :], alog_ref[...], dtb_ref[...]) for r in rows]
    gcol = [_dot_hi(tri, g) for g, _ in gate]
    grow = [_dot_nt_hi(g.T, tri) for g, _ in gate]
    terms = []
    for group in groups:
        units = [(j, h) for j in group for h in heads]
        terms.append(_delta_chunk_terms(
            [q_ref[rows[j], cols[h]] for j, h in units], [k_ref[rows[j], cols[h]] for j, h in units],
            [v_ref[rows[j], cols[h]] for j, h in units], [gcol[j][:, h:h + 1] for j, h in units],
            [grow[j][h:h + 1, :] for j, h in units],
            [gate[j][1][:, DN_HEADS + h:DN_HEADS + h + 1] for j, h in units], masks))
    u, w, qk, qd, kd, gl = (sum((list(t[i]) for t in terms), []) for i in range(6))
    s = [s_ref[h] for h in heads]
    for j in chunks:
        at = lambda xs: xs[j * DN_HEADS:(j + 1) * DN_HEADS]
        v_new = _each(jnp.subtract, at(u), _each(_bdot, at(w), s))
        o = _each(jnp.add, _each(_bdot, at(qd), s), _each(_bdot, at(qk), v_new))
        s = _each(lambda ss, g, x: ss * g + x, s, at(gl), _each(_bdot_tn, at(kd), v_new))
        for h in heads:
            on = o[h] * lax.rsqrt(jnp.mean(o[h] * o[h], -1, keepdims=True) + EPS) * ng_ref[...]
            o_ref[rows[j], cols[h]] = (on * _silu(zg_ref[rows[j], cols[h]])).astype(BF16)
    for h in heads:
        s_ref[h] = s[h]

    @pl.when(c == pl.num_programs(1) - 1)
    def _():
        sfin_ref[...] = s_ref[...]


def _delta_prompt(z3, qkv_col, zg_col, ab3, lyr, n_lyr, conv_w, a_log, dt_bias, norm_g, carried):
    b, t_len, _ = z3.shape
    ch = DN_HEADS * DN_DK
    n = DN_CHUNK * DN_STEP_CHUNKS
    zcol = lambda col: pl.BlockSpec((None, n, ch), lambda i, c: (i, c, col))
    stacked = {1: ((None, DN_HEADS, DN_DK, DN_DK), lambda i, c: (i, 0, 0, 0))}
    call, extra = _stacked_call(
        _delta_prompt_kernel, 9, carried, stacked, lyr,
        grid=(b, t_len // n),
        in_specs=[
            zcol(qkv_col), zcol(qkv_col + 1), zcol(qkv_col + 2), zcol(zg_col),
            pl.BlockSpec((None, n, ab3.shape[-1]), lambda i, c: (i, c, 0)),
            _whole(lyr, conv_w), _whole(lyr, a_log), _whole(lyr, dt_bias), _whole(lyr, norm_g),
        ],
        out_specs=[
            pl.BlockSpec((None, n, ch), lambda i, c: (i, c, 0)),
            None,
        ],
        out_shape=[jax.ShapeDtypeStruct((b, t_len, ch), BF16),
                   jax.ShapeDtypeStruct((n_lyr, b, DN_HEADS, DN_DK, DN_DK), F32)],
        scratch_shapes=[pltpu.VMEM((DN_HEADS, DN_DK, DN_DK), F32), pltpu.VMEM((3, DN_HALO, ch), F32),
                        pltpu.VMEM((DN_HALO + DN_CHUNK * DN_PREP_CHUNKS, ch), F32), pltpu.VMEM((3, n, ch), F32)],
        compiler_params=_params("parallel", "arbitrary"),
        name="delta_prompt",
    )
    return call(z3, z3, z3, z3, ab3, conv_w, a_log, dt_bias, norm_g, *extra)


def _odd_sample_kernel(p_ref, gt_ref, q_ref, k_ref, v_ref, zg_ref, ab_ref, cst_ref, dst_ref, s_ref,
                       dw_ref, db_ref, lg_ref, lb_ref, cw_ref, alog_ref, dtb_ref, ng_ref,
                       oc_ref, od_ref, ncst_ref, ndst_ref, ns_ref, obuf_ref):
    bt, ch = p_ref.shape
    hist = CONV_WIDTH - 1
    glu = p_ref[...] * jax.nn.sigmoid(gt_ref[...])
    acc = db_ref[...] + dw_ref[hist:hist + 1, :] * glu
    for k in range(hist):
        acc = acc + dw_ref[k:k + 1, :] * cst_ref[k]
    oc_ref[...] = _silu(_layernorm_rows(acc, lg_ref[...], lb_ref[...])).astype(BF16)
    ncst_ref[0:hist - 1] = cst_ref[1:hist]
    ncst_ref[hist - 1] = glu

    dh = DN_CONV - 1
    parts = []
    for pi, x_ref in enumerate((q_ref, k_ref, v_ref)):
        cols = slice(pi * ch, (pi + 1) * ch)
        x = x_ref[...]
        y = cw_ref[dh:dh + 1, cols] * x
        for k in range(dh):
            y = y + cw_ref[k:k + 1, cols] * dst_ref[k, :, cols]
        parts.append(_silu(y))
        ndst_ref[0:dh - 1, :, cols] = dst_ref[1:dh, :, cols]
        ndst_ref[dh - 1, :, cols] = x
    q = _l2norm_heads(parts[0], DN_DK ** -0.5)
    k = _l2norm_heads(parts[1], 1.0)
    v = parts[2]
    g, beta = _gate_terms(ab_ref[...], alog_ref[...], dtb_ref[...])
    eg_all = jnp.exp(g)

    assert 2 * DN_HEADS * bt == LANES
    kq = jnp.concatenate([x[:, h * DN_DK:(h + 1) * DN_DK] for x in (k, q) for h in range(DN_HEADS)], 0)
    kq_t = kq.T
    for h in range(DN_HEADS):
        cols = slice(h * DN_DK, (h + 1) * DN_DK)
        for b in range(bt):
            kc = kq_t[:, h * bt + b:h * bt + b + 1]
            qc = kq_t[:, (DN_HEADS + h) * bt + b:(DN_HEADS + h) * bt + b + 1]
            s = s_ref[b, h]
            sk = jnp.sum(s * kc, 0, keepdims=True)
            sq = jnp.sum(s * qc, 0, keepdims=True)
            eg = eg_all[b:b + 1, h:h + 1]
            bi = beta[b:b + 1, DN_HEADS + h:DN_HEADS + h + 1]
            v_new = bi * (v[b:b + 1, cols] - eg * sk)
            qk = jnp.sum(q[b:b + 1, cols] * k[b:b + 1, cols], -1, keepdims=True)
            obuf_ref[b:b + 1, cols] = eg * sq + qk * v_new
            ns_ref[b, h] = s * eg + kc * v_new
    for h in range(DN_HEADS):
        cols = slice(h * DN_DK, (h + 1) * DN_DK)
        o = obuf_ref[:, cols]
        o = o * lax.rsqrt(jnp.mean(o * o, -1, keepdims=True) + EPS) * ng_ref[...]
        od_ref[:, cols] = (o * _silu(zg_ref[:, cols])).astype(BF16)


def _odd_sample(z, ab, lyr, cstate, dstate, sstate, dw, db, ln_g, ln_b, conv_w, a_log, dt_bias, norm_g, carried, bt):
    n = z.shape[0]
    ch = db.shape[-1]
    zcol = lambda c: pl.BlockSpec((bt, ch), lambda i: (i, c))
    lead_blk = lambda a: ((bt,) + a.shape[2:], lambda i: (i,) + (0,) * (a.ndim - 2))
    hist_blk = lambda a: ((a.shape[1], bt, a.shape[3]), lambda i: (0, i, 0))
    lead = lambda a: _layer(lyr, *lead_blk(a))
    hist = lambda a: _layer(lyr, *hist_blk(a))
    act = jax.ShapeDtypeStruct((n, ch), BF16)
    same = lambda a: jax.ShapeDtypeStruct(a.shape, F32)
    stacked = {2: hist_blk(cstate), 3: hist_blk(dstate), 4: lead_blk(sstate)}
    call, extra = _stacked_call(
        _odd_sample_kernel, 18, carried, stacked, lyr,
        grid=(n // bt,),
        in_specs=[zcol(0), zcol(1), zcol(2), zcol(3), zcol(4), zcol(5),
                  pl.BlockSpec((bt, ab.shape[-1]), lambda i: (i, 0)),
                  hist(cstate), hist(dstate), lead(sstate),
                  _whole(lyr, dw), _whole(lyr, db), _whole(lyr, ln_g), _whole(lyr, ln_b),
                  _whole(lyr, conv_w), _whole(lyr, a_log), _whole(lyr, dt_bias), _whole(lyr, norm_g)],
        out_specs=[zcol(0), zcol(0), None, None, None],
        out_shape=[act, act, same(cstate), same(dstate), same(sstate)],
        scratch_shapes=[pltpu.VMEM((bt, ch), F32)],
        compiler_params=_params("parallel"),
        name="odd_sample",
    )
    return call(z, z, z, z, z, z, ab, cstate, dstate, sstate, dw, db, ln_g, ln_b, conv_w, a_log, dt_bias, norm_g,
                *extra)


def kernel(x_prompt, x_sample, state_pool, state_conv_c, state_dn_conv, state_dn_S, norm_mix, norm_ffn, norm_final, ev_w_in, pool_w, pool_scale, sgu_ln_g, sgu_ln_b, sgu_ws, sgu_b, ev_w_out, od_w_in, cv_dw, cv_db, cv_ln_g, cv_ln_b, dn_conv_w, dn_a_log, dn_dt_bias, dn_norm_g, od_w_out, ffn_w_up, ffn_w_down):
    bp, t_len, d = x_prompt.shape
    bs = x_sample.shape[0]
    assert x_sample.shape[1] == 1
    depth = norm_mix.shape[0]
    n_even, n_odd = ev_w_in.shape[0], od_w_in.shape[0]
    ch = pool_scale.shape[1]
    odd_main = 6 * ch
    n_gate = 2 * DN_HEADS
    assert od_w_in.shape[2] == odd_main + n_gate and odd_main % n_gate == 0

    rows = lambda a: a.reshape(a.shape[0], 1, a.shape[-1])
    gate_rows = lambda a: rows(jnp.pad(a, ((0, 0), (0, DN_HEADS))))
    od_in = jnp.swapaxes(od_w_in, 1, 2)
    pool_wb = pool_w.astype(BF16)
    g_mix, g_ffn, g_fin = rows(norm_mix), rows(norm_ffn), norm_final.reshape(1, d)
    p_scale, s_lg, s_lb = rows(pool_scale), rows(sgu_ln_g), rows(sgu_ln_b)
    s_bt = jnp.swapaxes(sgu_b, 1, 2)
    s_w0 = rows(jnp.repeat(sgu_ws[:, :, 0, 0], ch // SGU_HEADS, axis=1))
    s_b0 = rows(jnp.repeat(sgu_b[:, :, 0], ch // SGU_HEADS, axis=1))
    c_db, c_lg, c_lb = rows(cv_db), rows(cv_ln_g), rows(cv_ln_b)
    a_log, dt_bias, d_ng = gate_rows(dn_a_log), gate_rows(dn_dt_bias), rows(dn_norm_g)

    hist_major = lambda a: jnp.swapaxes(a, 1, 2)
    st_pool, st_cc, st_dc = hist_major(state_pool), hist_major(state_conv_c), hist_major(state_dn_conv)

    xp = x_prompt.reshape(bp * t_len, d)
    xs = x_sample.reshape(bs, d)
    tn = 2 * _tiles(bp * t_len)[2]
    flat = lambda a: a.reshape(bp * t_len, a.shape[-1])
    pool_p, dc_p = [], []
    even_s = cc_p = s_p = odd_s = None
    for l in range(depth):
        i = l // 2
        if l % 2 == 0:
            zs, w_in = _norm_matmul(xs, g_mix, l, ev_w_in, i, 3 * ch, tn, emit_weights=True)
            zp, = _norm_matmul(xp, g_mix, l, w_in, 0, 3 * ch, tn)
            zp3 = zp.reshape(bp, t_len, 3 * ch)
            a_s, b_s, *even_s = _even_sample(zs, i, st_pool, pool_wb, p_scale, s_lg, s_lb, s_w0, s_b0, even_s, 8)
            pool_p.append(zp3[:, t_len - POOL_HIST:, :ch])
            xs, w_out = _mix_out(a_s, b_s, ev_w_out, i, xs, emit_weights=True)
            xp = flat(_even_prompt(zp3, xp.reshape(bp, t_len, d), i, pool_wb, p_scale, s_lg, s_lb, sgu_ws, s_bt,
                                   w_out, 512))
        else:
            zs, abs_, w_in, w_gate = _norm_matmul(xs, g_mix, l, od_in, i, odd_main, tn, True,
                                                  (od_in, i, odd_main // n_gate, n_gate), emit_weights=True)
            zp, abp = _norm_matmul(xp, g_mix, l, w_in, 0, odd_main, tn, True, (w_gate, 0, 0, n_gate))
            zp3 = zp.reshape(bp, t_len, odd_main)
            c_p, *cc_p = _conv_prompt(zp3, i, n_odd, cv_dw, c_db, c_lg, c_lb, cc_p, 512)
            d_p, *s_p = _delta_prompt(zp3, 2, 5, abp.reshape(bp, t_len, n_gate), i, n_odd, dn_conv_w, a_log, dt_bias,
                                      d_ng, s_p)
            c_s, d_s, *odd_s = _odd_sample(zs, abs_, i, st_cc, st_dc, state_dn_S, cv_dw, c_db, c_lg,
                                           c_lb, dn_conv_w, a_log, dt_bias, d_ng, odd_s, 8)
            dc_p.append(zp3[:, t_len - (DN_CONV - 1):, 2 * ch:5 * ch])
            xs, w_out = _mix_out(c_s, d_s, od_w_out, i, xs, emit_weights=True)
            xp = _mix_out(flat(c_p), flat(d_p), w_out, 0, xp)
        last = l == depth - 1
        xs, = _ffn(xs, g_ffn, l, (ffn_w_up, ffn_w_down), g_fin, last)
        xp, = _ffn(xp, g_ffn, l, (ffn_w_up, ffn_w_down), g_fin, last)
    pool_s, v_s = even_s
    cc_s, dc_s, s_s = odd_s
    return (xp.reshape(bp, t_len, d), xs.reshape(bs, 1, d), jnp.stack(pool_p), hist_major(pool_s),
            v_s.reshape(n_even, bs, 1, ch), cc_p[0], hist_major(cc_s), jnp.stack(dc_p), hist_major(dc_s), s_p[0], s_s)
```
